```python
import jax, jax.numpy as jnp
from jax import lax
import numpy as np

D_MODEL = 2048
BATCH = 16
SEQ = 2048
DEPTH = 1

D_MIX = D_MODEL
HEAD_CH = 64
D_A = D_MIX // 2
D_B = D_MIX - D_A
D_IN_PROJ = 3 * D_A + 2 * D_B
K_SHORT = 3
K_CONF = 31
N_EXPERTS = 64
TOP_K = 8
N_EXPERT_GROUPS = 8
TOPK_GROUPS = 4
D_EXPERT = D_MODEL // 4
D_SHARED = D_EXPERT
ROUTED_SCALE = 2.5
DISPATCH_BLOCK = 256
EPS = 1e-6

kernel_name = "hybrid_shortconv_conformer_moe_adaln"


def _rmsnorm(x, g):
    xf = x.astype(jnp.float32)
    y = xf * lax.rsqrt(jnp.mean(xf * xf, axis=-1, keepdims=True) + EPS)
    return (y * g.astype(jnp.float32)).astype(x.dtype)


def _head_rmsnorm(y, g):
    shp = y.shape
    yf = y.astype(jnp.float32).reshape(shp[:-1] + (shp[-1] // HEAD_CH, HEAD_CH))
    yf = yf * lax.rsqrt(jnp.mean(yf * yf, axis=-1, keepdims=True) + EPS)
    return (yf.reshape(shp) * g.astype(jnp.float32)).astype(y.dtype)


def _layernorm(x, g, b):
    xf = x.astype(jnp.float32)
    mu = jnp.mean(xf, axis=-1, keepdims=True)
    var = jnp.mean(jnp.square(xf - mu), axis=-1, keepdims=True)
    y = (xf - mu) * lax.rsqrt(var + EPS)
    return (y * g.astype(jnp.float32) + b.astype(jnp.float32)).astype(x.dtype)


def _causal_depthwise_conv(u, w):
    k, ch = w.shape
    return lax.conv_general_dilated(
        u, w[:, None, :].astype(u.dtype), window_strides=(1,), padding=[(k - 1, 0)],
        dimension_numbers=("NWC", "WIO", "NWC"), feature_group_count=ch)


def _modulate(h, shift, scale):
    return h * (1.0 + scale[:, None, :]) + shift[:, None, :]


def _mixer(h, w_in, conv_a_w, conv_b_w, conv_b_b, ln_b_g, ln_b_b, head_norm_a_g, head_norm_b_g, w_out):
    proj = jnp.einsum("bsd,de->bse", h, w_in)
    xa, ba, ca, vb, gb = jnp.split(proj, [D_A, 2 * D_A, 3 * D_A, 3 * D_A + D_B], axis=-1)
    ya = ba * _causal_depthwise_conv(ca * xa, conv_a_w)
    glu = vb * jax.nn.sigmoid(gb)
    zb = _causal_depthwise_conv(glu, conv_b_w) + conv_b_b
    zb = jax.nn.silu(_layernorm(zb, ln_b_g, ln_b_b))
    y = jnp.concatenate([_head_rmsnorm(ya, head_norm_a_g), _head_rmsnorm(zb, head_norm_b_g)], axis=-1)
    return jnp.einsum("bse,ed->bsd", y, w_out)


def _swiglu(x, wg, wu, wd):
    return (jax.nn.silu(x @ wg) * (x @ wu)) @ wd


def _moe(h, w_router, router_bias, w_gate, w_up, w_down, ws_gate, ws_up, ws_down):
    bsz, seq, d = h.shape
    n_tok = bsz * seq
    hf = h.reshape(n_tok, d)
    scores = jax.nn.sigmoid(hf.astype(jnp.float32) @ w_router.astype(jnp.float32))
    biased = scores + router_bias.astype(jnp.float32)
    grp = biased.reshape(n_tok, N_EXPERT_GROUPS, N_EXPERTS // N_EXPERT_GROUPS)
    grp_score = lax.top_k(grp, 2)[0].sum(-1)
    _, top_grp = lax.top_k(grp_score, TOPK_GROUPS)
    grp_mask = jnp.any(top_grp[:, :, None] == jnp.arange(N_EXPERT_GROUPS)[None, None, :], axis=1)
    exp_mask = jnp.repeat(grp_mask, N_EXPERTS // N_EXPERT_GROUPS, axis=1)
    _, idx = lax.top_k(jnp.where(exp_mask, biased, -jnp.inf), TOP_K)
    wts = jnp.take_along_axis(scores, idx, axis=1)
    wts = (wts / jnp.sum(wts, axis=-1, keepdims=True) * ROUTED_SCALE).astype(h.dtype)

    n_assign = n_tok * TOP_K
    n_blocks = (n_assign + N_EXPERTS * (DISPATCH_BLOCK - 1) + DISPATCH_BLOCK - 1) // DISPATCH_BLOCK
    flat_e = idx.reshape(-1)
    flat_tok = jnp.arange(n_assign, dtype=jnp.int32) // TOP_K
    flat_w = wts.reshape(-1)
    order = jnp.argsort(flat_e)
    e_sorted = flat_e[order]
    counts = jnp.bincount(flat_e, length=N_EXPERTS)
    start = jnp.cumsum(counts) - counts
    padded = (counts + DISPATCH_BLOCK - 1) // DISPATCH_BLOCK * DISPATCH_BLOCK
    pend = jnp.cumsum(padded)
    pstart = pend - padded
    dest = pstart[e_sorted] + jnp.arange(n_assign) - start[e_sorted]
    n_rows = n_blocks * DISPATCH_BLOCK
    buf_tok = jnp.zeros((n_rows,), jnp.int32).at[dest].set(flat_tok[order])
    buf_w = jnp.zeros((n_rows,), h.dtype).at[dest].set(flat_w[order])
    block_e = jnp.minimum(
        jnp.searchsorted(pend, jnp.arange(n_blocks) * DISPATCH_BLOCK, side="right"), N_EXPERTS - 1)

    def body(acc, blk):
        tok, wt, e = blk
        yb = _swiglu(hf[tok], w_gate[e], w_up[e], w_down[e])
        return acc.at[tok].add(yb * wt[:, None]), None

    routed, _ = lax.scan(body, jnp.zeros_like(hf),
                         (buf_tok.reshape(n_blocks, DISPATCH_BLOCK),
                          buf_w.reshape(n_blocks, DISPATCH_BLOCK), block_e))
    shared = _swiglu(hf, ws_gate, ws_up, ws_down)
    return (routed + shared).reshape(bsz, seq, d)


def setup_inputs(seed: int = 0) -> dict:
    key = jax.random.key(seed)
    ks = jax.random.split(key, 24)
    L, D, E, F = DEPTH, D_MODEL, N_EXPERTS, D_EXPERT
    nrm = lambda k, shp, s: jax.random.normal(k, shp, jnp.float32) * s
    gain = lambda k, shp: 1.0 + 0.02 * jax.random.normal(k, shp, jnp.float32)
    return {
        "x": nrm(ks[0], (BATCH, SEQ, D), 1.0),
        "c": nrm(ks[1], (BATCH, D), 1.0),
        "w_ada": nrm(ks[2], (L, D, 6 * D), 0.5 * D ** -0.5),
        "b_ada": nrm(ks[3], (L, 6 * D), 0.02),
        "norm_mix_g": gain(ks[4], (L, D)),
        "w_in": nrm(ks[5], (L, D, D_IN_PROJ), D ** -0.5),
        "conv_a_w": nrm(ks[6], (L, K_SHORT, D_A), K_SHORT ** -0.5),
        "conv_b_w": nrm(ks[7], (L, K_CONF, D_B), K_CONF ** -0.5),
        "conv_b_b": nrm(ks[8], (L, D_B), 0.02),
        "ln_b_g": gain(ks[9], (L, D_B)),
        "ln_b_b": nrm(ks[10], (L, D_B), 0.02),
        "head_norm_a_g": gain(ks[11], (L, D_A)),
        "head_norm_b_g": gain(ks[12], (L, D_B)),
        "w_out": nrm(ks[13], (L, D_MIX, D), D_MIX ** -0.5),
        "norm_ffn_g": gain(ks[14], (L, D)),
        "w_router": nrm(ks[15], (L, D, E), D ** -0.5),
        "router_bias": nrm(ks[16], (L, E), 0.01),
        "w_gate": nrm(ks[17], (L, E, D, F), D ** -0.5),
        "w_up": nrm(ks[18], (L, E, D, F), D ** -0.5),
        "w_down": nrm(ks[19], (L, E, F, D), F ** -0.5),
        "w_shared_gate": nrm(ks[20], (L, D, D_SHARED), D ** -0.5),
        "w_shared_up": nrm(ks[21], (L, D, D_SHARED), D ** -0.5),
        "w_shared_down": nrm(ks[22], (L, D_SHARED, D), D_SHARED ** -0.5),
        "norm_final_g": gain(ks[23], (D,)),
    }


def reference(x, c, w_ada, b_ada, norm_mix_g, w_in, conv_a_w, conv_b_w, conv_b_b, ln_b_g, ln_b_b,
              head_norm_a_g, head_norm_b_g, w_out, norm_ffn_g, w_router, router_bias, w_gate, w_up,
              w_down, w_shared_gate, w_shared_up, w_shared_down, norm_final_g):
    c_act = jax.nn.silu(c)
    for l in range(DEPTH):
        mod = c_act @ w_ada[l] + b_ada[l]
        sh_m, sc_m, g_m, sh_f, sc_f, g_f = jnp.split(mod, 6, axis=-1)
        h = _modulate(_rmsnorm(x, norm_mix_g[l]), sh_m, sc_m)
        x = x + g_m[:, None, :] * _mixer(h, w_in[l], conv_a_w[l], conv_b_w[l], conv_b_b[l], ln_b_g[l],
                                         ln_b_b[l], head_norm_a_g[l], head_norm_b_g[l], w_out[l])
        h = _modulate(_rmsnorm(x, norm_ffn_g[l]), sh_f, sc_f)
        x = x + g_f[:, None, :] * _moe(h, w_router[l], router_bias[l], w_gate[l], w_up[l], w_down[l],
                                       w_shared_gate[l], w_shared_up[l], w_shared_down[l])
    return _rmsnorm(x, norm_final_g)
```

```python
import functools

import jax
import jax.numpy as jnp
from jax import lax
from jax.experimental import pallas as pl
from jax.experimental.pallas import tpu as pltpu
from jax.experimental.pallas import tpu_sc as plsc

F32 = jnp.float32
BF16 = jnp.bfloat16
I32 = jnp.int32

HEAD_CH = 64
K_SHORT = 3
K_CONF = 31
N_EXPERTS = 64
TOP_K = 8
N_GROUPS = 8
TOPK_GROUPS = 4
GROUP_SIZE = N_EXPERTS // N_GROUPS
ROUTED_SCALE = 2.5
EPS = 1e-6

LANES = 128
HALO = 32
ROW_BLOCK = 256
VMEM_LIMIT = 56 * 1024 * 1024
HI_MASK = -65536
SC_CHUNK = 32


def _cparams(*sem):
    return pltpu.CompilerParams(dimension_semantics=sem, vmem_limit_bytes=VMEM_LIMIT)


def _dot(a, b):
    return jnp.dot(a, b, preferred_element_type=F32)


def _split_bf16(x):
    hi = x.astype(BF16)
    lo = (x - hi.astype(F32)).astype(BF16)
    return hi, lo


def _pack_halves(lo, hi):
    lo_b = lax.bitcast_convert_type(lo.astype(BF16).astype(F32), I32)
    hi_b = lax.bitcast_convert_type(hi.astype(BF16).astype(F32), I32)
    return lax.shift_right_logical(lo_b, 16) | (hi_b & HI_MASK)


def _unpack_halves(p):
    lo = lax.bitcast_convert_type(lax.shift_left(p, 16), F32)
    hi = lax.bitcast_convert_type(p & HI_MASK, F32)
    return lo, hi


def _adaln_kernel(c_ref, w_ref, b_ref, o_ref):
    c = c_ref[...]
    ca = c * jax.nn.sigmoid(c)
    chi, clo = _split_bf16(ca)
    whi, wlo = _split_bf16(w_ref[...])
    o_ref[...] = _dot(chi, whi) + _dot(clo, whi) + _dot(chi, wlo) + b_ref[...]


def _adaln(c, w, b):
    bsz, d = c.shape
    n = w.shape[1]
    tn = 1024
    return pl.pallas_call(
        _adaln_kernel,
        grid=(n // tn,),
        in_specs=[pl.BlockSpec((bsz, d), lambda j: (0, 0)),
                  pl.BlockSpec((d, tn), lambda j: (0, j)),
                  pl.BlockSpec((1, tn), lambda j: (0, j))],
        out_specs=pl.BlockSpec((bsz, tn), lambda j: (0, j)),
        out_shape=jax.ShapeDtypeStruct((bsz, n), F32),
        compiler_params=_cparams("arbitrary"),
        name="adaln",
    )(c, w, b.reshape(1, n))


def _inproj_kernel(x_ref, sh_ref, sc_ref, g_ref, w_ref, o_ref, h_scr):
    @pl.when(pl.program_id(1) == 0)
    def _():
        x = x_ref[...]
        ms = jnp.mean(x * x, axis=-1, keepdims=True)
        y = x * lax.rsqrt(ms + EPS) * g_ref[...]
        h_scr[...] = (y * (1.0 + sc_ref[0]) + sh_ref[0]).astype(BF16)

    o_ref[...] = _dot(h_scr[...], w_ref[...]).astype(BF16)


def _inproj(xf, sh, sc, g, w_bf, seq):
    t, d = xf.shape
    n = w_bf.shape[1]
    tm, tn = 1024, 1024
    per_seq = seq // tm
    vec = pl.BlockSpec((1, 1, d), lambda i, j: (i // per_seq, 0, 0))
    return pl.pallas_call(
        _inproj_kernel,
        grid=(t // tm, n // tn),
        in_specs=[pl.BlockSpec((tm, d), lambda i, j: (i, 0)), vec, vec,
                  pl.BlockSpec((1, d), lambda i, j: (0, 0)),
                  pl.BlockSpec((d, tn), lambda i, j: (0, j))],
        out_specs=pl.BlockSpec((tm, tn), lambda i, j: (i, j)),
        out_shape=jax.ShapeDtypeStruct((t, n), BF16),
        scratch_shapes=[pltpu.VMEM((tm, d), BF16)],
        compiler_params=_cparams("arbitrary", "arbitrary"),
        name="inproj",
    )(xf, sh, sc, g, w_bf)


def _mixer_kernel(p_ref, halo_ref, x_ref, gm_ref, wa_ref, wb_ref, bb_ref, lng_ref, lnb_ref, hg_ref,
                  grp_ref, grpt_ref, wout_ref, o_ref, ua_scr, ub_scr):
    ts = p_ref.shape[0]
    da = wa_ref.shape[1]
    keep = (pl.program_id(1) > 0).astype(F32)

    def seg(ref, k):
        return ref[:, k * da:(k + 1) * da].astype(F32)

    ua_scr[0:HALO, :] = seg(halo_ref, 2) * seg(halo_ref, 0) * keep
    ua_scr[HALO:HALO + ts, :] = seg(p_ref, 2) * seg(p_ref, 0)
    ub_scr[0:HALO, :] = seg(halo_ref, 3) * jax.nn.sigmoid(seg(halo_ref, 4)) * keep
    ub_scr[HALO:HALO + ts, :] = seg(p_ref, 3) * jax.nn.sigmoid(seg(p_ref, 4))

    conv_a = jnp.zeros((ts, da), F32)
    for k in range(K_SHORT):
        off = HALO - (K_SHORT - 1) + k
        conv_a = conv_a + wa_ref[k:k + 1, :] * ua_scr[off:off + ts, :]
    ya = seg(p_ref, 1) * conv_a

    zb = jnp.zeros((ts, da), F32) + bb_ref[...]
    for k in range(K_CONF):
        off = HALO - (K_CONF - 1) + k
        zb = zb + wb_ref[k:k + 1, :] * ub_scr[off:off + ts, :]
    mu = jnp.mean(zb, axis=-1, keepdims=True)
    zc = zb - mu
    var = jnp.mean(zc * zc, axis=-1, keepdims=True)
    zn = zc * lax.rsqrt(var + EPS) * lng_ref[...] + lnb_ref[...]
    zs = zn * jax.nn.sigmoid(zn)

    y = jnp.concatenate([ya, zs], axis=-1)
    gsum = _dot((y * y).astype(BF16), grp_ref[...])
    scale = lax.rsqrt(gsum * (1.0 / HEAD_CH) + EPS)
    s_hi, s_lo = _split_bf16(scale)
    scale_full = _dot(s_hi, grpt_ref[...]) + _dot(s_lo, grpt_ref[...])
    yn = (y * scale_full * hg_ref[...]).astype(BF16)
    o_ref[...] = x_ref[...] + gm_ref[0] * _dot(yn, wout_ref[...])


def _mixer(proj, xf, g_m, conv_a_w, conv_b_w, conv_b_b, ln_g, ln_b, head_g, w_out_bf, bsz, seq):
    t, d = xf.shape
    n_in = proj.shape[1]
    da = conv_a_w.shape[1]
    ts = 256
    per_seq = seq // ts
    halo_per_tile = ts // HALO
    n_heads = d // HEAD_CH
    head_of = jnp.arange(d, dtype=I32) // HEAD_CH
    grp = (head_of[:, None] == jnp.arange(n_heads, dtype=I32)[None, :]).astype(BF16)
    const = lambda shape: pl.BlockSpec(shape, lambda b, s: (0,) * len(shape))
    return pl.pallas_call(
        _mixer_kernel,
        grid=(bsz, per_seq),
        in_specs=[
            pl.BlockSpec((ts, n_in), lambda b, s: (b * per_seq + s, 0)),
            pl.BlockSpec((HALO, n_in), lambda b, s: (jnp.maximum((b * per_seq + s) * halo_per_tile - 1, 0), 0)),
            pl.BlockSpec((ts, d), lambda b, s: (b * per_seq + s, 0)),
            pl.BlockSpec((1, 1, d), lambda b, s: (b, 0, 0)),
            const((K_SHORT, da)), const((K_CONF, da)), const((1, da)), const((1, da)), const((1, da)),
            const((1, d)), const((d, n_heads)), const((n_heads, d)), const((d, d)),
        ],
        out_specs=pl.BlockSpec((ts, d), lambda b, s: (b * per_seq + s, 0)),
        out_shape=jax.ShapeDtypeStruct((t, d), F32),
        scratch_shapes=[pltpu.VMEM((HALO + ts, da), F32), pltpu.VMEM((HALO + ts, da), F32)],
        compiler_params=_cparams("arbitrary", "arbitrary"),
        name="mixer",
    )(proj, proj, xf, g_m, conv_a_w, conv_b_w, conv_b_b.reshape(1, da), ln_g.reshape(1, da),
      ln_b.reshape(1, da), head_g.reshape(1, d), grp, grp.T, w_out_bf)


def _first_argmax(vals, iota, size):
    m = jnp.max(vals, axis=0, keepdims=True)
    idx = jnp.min(jnp.where(vals == m, iota, size), axis=0, keepdims=True)
    return m, idx


def _ffn_pre_kernel(x_ref, sh_ref, sc_ref, gf_ref, g_ref, wrh_ref, wrl_ref, rb_ref, tri_ref,
                    wsg_ref, wsu_ref, wsd_ref,
                    base_ref, hp_ref, idx_ref, rank_ref, wt_ref, cnt_ref, carry_scr):
    tm, d = x_ref.shape
    half = d // 2

    @pl.when(pl.program_id(0) == 0)
    def _():
        carry_scr[...] = jnp.zeros_like(carry_scr)

    x = x_ref[...]
    ms = jnp.mean(x * x, axis=-1, keepdims=True)
    h = x * lax.rsqrt(ms + EPS) * g_ref[...] * (1.0 + sc_ref[0]) + sh_ref[0]
    h_hi, h_lo = _split_bf16(h)
    hp_ref[...] = _pack_halves(h[:, :half], h[:, half:])

    nt = (((1,), (1,)), ((), ()))
    logits = (lax.dot_general(wrh_ref[...], h_hi, nt, preferred_element_type=F32)
              + lax.dot_general(wrh_ref[...], h_lo, nt, preferred_element_type=F32)
              + lax.dot_general(wrl_ref[...], h_hi, nt, preferred_element_type=F32))
    scores = jax.nn.sigmoid(logits)
    biased = scores + rb_ref[...]

    b3 = biased.reshape(N_GROUPS, GROUP_SIZE, tm)
    sub_iota = lax.broadcasted_iota(I32, (N_GROUPS, GROUP_SIZE, tm), 1)
    m1 = jnp.max(b3, axis=1, keepdims=True)
    i1 = jnp.min(jnp.where(b3 == m1, sub_iota, GROUP_SIZE), axis=1, keepdims=True)
    m2 = jnp.max(jnp.where(sub_iota == i1, -jnp.inf, b3), axis=1, keepdims=True)
    gscore = (m1 + m2).reshape(N_GROUPS, tm)

    g_iota = lax.broadcasted_iota(I32, (N_GROUPS, tm), 0)
    gsel = jnp.zeros((N_GROUPS, tm), jnp.bool_)
    gwork = gscore
    for _ in range(TOPK_GROUPS):
        _, gi = _first_argmax(gwork, g_iota, N_GROUPS)
        hit = g_iota == gi
        gsel = gsel | hit
        gwork = jnp.where(hit, -jnp.inf, gwork)
    emask = jnp.broadcast_to(gsel.reshape(N_GROUPS, 1, tm), (N_GROUPS, GROUP_SIZE, tm)).reshape(N_EXPERTS, tm)

    e_iota = lax.broadcasted_iota(I32, (N_EXPERTS, tm), 0)
    work = jnp.where(emask, biased, -jnp.inf)
    chosen = jnp.zeros((N_EXPERTS, tm), jnp.bool_)
    idx_rows, w_rows = [], []
    for _ in range(TOP_K):
        _, ei = _first_argmax(work, e_iota, N_EXPERTS)
        hit = e_iota == ei
        chosen = chosen | hit
        work = jnp.where(hit, -jnp.inf, work)
        idx_rows.append(ei)
        w_rows.append(jnp.sum(jnp.where(hit, scores, 0.0), axis=0, keepdims=True))
    idx = jnp.concatenate(idx_rows, axis=0)
    w = jnp.concatenate(w_rows, axis=0)
    w = w / jnp.sum(w, axis=0, keepdims=True) * ROUTED_SCALE

    onehot = chosen.astype(BF16)
    before = _dot(onehot, tri_ref[...]) + carry_scr[:, 0:1]
    rank_rows = [jnp.sum(jnp.where(e_iota == idx_rows[k], before, 0.0), axis=0, keepdims=True)
                 for k in range(TOP_K)]
    carry_new = carry_scr[...] + jnp.sum(chosen.astype(F32), axis=1, keepdims=True)
    carry_scr[...] = carry_new
    cnt_ref[...] = carry_new.astype(I32)
    idx_ref[...] = idx
    rank_ref[...] = jnp.concatenate(rank_rows, axis=0).astype(I32)
    w_pad = jnp.concatenate([w, jnp.zeros((LANES - TOP_K, tm), F32)], axis=0)
    wt_ref[...] = w_pad.T

    hb = h_hi
    act = jax.nn.silu(_dot(hb, wsg_ref[...])) * _dot(hb, wsu_ref[...])
    base_ref[...] = x + gf_ref[0] * _dot(act.astype(BF16), wsd_ref[...])


def _ffn_pre(x1, sh, sc, gf, g, w_router, router_bias, wsg, wsu, wsd, seq):
    t, d = x1.shape
    f = wsg.shape[1]
    tm = 512
    per_seq = seq // tm
    wr_hi, wr_lo = _split_bf16(w_router.T)
    tri = (jnp.arange(tm, dtype=I32)[:, None] < jnp.arange(tm, dtype=I32)[None, :]).astype(BF16)
    vec = pl.BlockSpec((1, 1, d), lambda i: (i // per_seq, 0, 0))
    const = lambda shape: pl.BlockSpec(shape, lambda i: (0,) * len(shape))
    return pl.pallas_call(
        _ffn_pre_kernel,
        grid=(t // tm,),
        in_specs=[pl.BlockSpec((tm, d), lambda i: (i, 0)), vec, vec, vec, const((1, d)),
                  const((N_EXPERTS, d)), const((N_EXPERTS, d)), const((N_EXPERTS, 1)), const((tm, tm)),
                  const((d, f)), const((d, f)), const((f, d))],
        out_specs=[pl.BlockSpec((tm, d), lambda i: (i, 0)),
                   pl.BlockSpec((tm, d // 2), lambda i: (i, 0)),
                   pl.BlockSpec((TOP_K, tm), lambda i: (0, i)),
                   pl.BlockSpec((TOP_K, tm), lambda i: (0, i)),
                   pl.BlockSpec((tm, LANES), lambda i: (i, 0)),
                   const((N_EXPERTS, LANES))],
        out_shape=[jax.ShapeDtypeStruct((t, d), F32),
                   jax.ShapeDtypeStruct((t, d // 2), I32),
                   jax.ShapeDtypeStruct((TOP_K, t), I32),
                   jax.ShapeDtypeStruct((TOP_K, t), I32),
                   jax.ShapeDtypeStruct((t, LANES), F32),
                   jax.ShapeDtypeStruct((N_EXPERTS, LANES), I32)],
        scratch_shapes=[pltpu.VMEM((N_EXPERTS, LANES), F32)],
        compiler_params=_cparams("arbitrary"),
        name="ffn_pre",
    )(x1, sh, sc, gf, g, wr_hi, wr_lo, router_bias.reshape(N_EXPERTS, 1), tri,
      wsg.astype(BF16), wsu.astype(BF16), wsd.astype(BF16))


def _dest_kernel(pstart_ref, idx_ref, rank_ref, o_ref):
    idx = idx_ref[...]
    acc = rank_ref[...]
    for e in range(N_EXPERTS):
        acc = acc + jnp.where(idx == e, pstart_ref[e], 0)
    o_ref[...] = acc


def _dest(pstart, idx, rank):
    k, t = idx.shape
    tl = min(4096, t)
    return pl.pallas_call(
        _dest_kernel,
        grid_spec=pltpu.PrefetchScalarGridSpec(
            num_scalar_prefetch=1,
            grid=(t // tl,),
            in_specs=[pl.BlockSpec((k, tl), lambda i, ps: (0, i)),
                      pl.BlockSpec((k, tl), lambda i, ps: (0, i))],
            out_specs=pl.BlockSpec((k, tl), lambda i, ps: (0, i))),
        out_shape=jax.ShapeDtypeStruct((k, t), I32),
        compiler_params=_cparams("arbitrary"),
        name="dest",
    )(pstart, idx, rank)


def _sc_mesh():
    return plsc.VectorSubcoreMesh(core_axis_name="c", subcore_axis_name="s")


def _sc_workers():
    info = plsc.get_sparse_core_info()
    return info.num_cores, info.num_cores * info.num_subcores


def _dispatch_rows(hp, dest3, n_rows):
    t, w = hp.shape
    n_chunks, top_k, chunk = dest3.shape
    n_cores, n_workers = _sc_workers()
    per_worker = n_chunks // n_workers

    @functools.partial(
        pl.kernel, mesh=_sc_mesh(),
        out_type=jax.ShapeDtypeStruct((n_rows, w), hp.dtype),
        scratch_types=[pltpu.VMEM((top_k, chunk), I32), pltpu.VMEM((chunk, w), hp.dtype),
                       pltpu.SemaphoreType.DMA],
        name="dispatch_rows")
    def k(hp_hbm, dest_hbm, xs_hbm, idx_v, rows_v, sem):
        wid = lax.axis_index("s") * n_cores + lax.axis_index("c")

        @pl.loop(0, per_worker)
        def _(j):
            c = wid * per_worker + j
            pltpu.sync_copy(dest_hbm.at[c], idx_v)
            pltpu.sync_copy(hp_hbm.at[pl.ds(c * chunk, chunk)], rows_v)
            copies = [pltpu.async_copy(rows_v, xs_hbm.at[idx_v.at[q]], sem) for q in range(top_k)]
            for cp in copies:
                cp.wait()

    return k(hp, dest3)


def _gather_rows(ys, flat_idx):
    n = flat_idx.shape[0]
    w = ys.shape[1]
    n_cores, n_workers = _sc_workers()
    per_worker = n // n_workers
    steps = per_worker // SC_CHUNK

    @functools.partial(
        pl.kernel, mesh=_sc_mesh(),
        out_type=jax.ShapeDtypeStruct((n, w), ys.dtype),
        scratch_types=[pltpu.VMEM((SC_CHUNK,), I32), pltpu.VMEM((SC_CHUNK, w), ys.dtype),
                       pltpu.SemaphoreType.DMA],
        name="gather_rows")
    def k(ys_hbm, idx_hbm, out_hbm, idx_v, rows_v, sem):
        wid = lax.axis_index("s") * n_cores + lax.axis_index("c")

        @pl.loop(0, steps)
        def _(j):
            base = wid * per_worker + j * SC_CHUNK
            pltpu.sync_copy(idx_hbm.at[pl.ds(base, SC_CHUNK)], idx_v)
            pltpu.async_copy(ys_hbm.at[idx_v], rows_v, sem).wait()
            pltpu.sync_copy(rows_v, out_hbm.at[pl.ds(base, SC_CHUNK)])

    return k(ys, flat_idx)


def _experts_kernel(be_ref, nu_ref, x_ref, wg_ref, wu_ref, wd_ref, o_ref, wg_scr, wu_scr, wd_scr):
    i = pl.program_id(0)
    prev = be_ref[jnp.maximum(i - 1, 0)]
    fresh = jnp.logical_or(i == 0, be_ref[i] != prev)

    @pl.when(fresh)
    def _():
        wg_scr[...] = wg_ref[...].astype(BF16)
        wu_scr[...] = wu_ref[...].astype(BF16)
        wd_scr[...] = wd_ref[...].astype(BF16)

    @pl.when(i < nu_ref[0])
    def _():
        half = wg_scr.shape[0] // 2
        lo, hi = _unpack_halves(x_ref[...])
        lo = lo.astype(BF16)
        hi = hi.astype(BF16)
        g = _dot(lo, wg_scr[0:half, :]) + _dot(hi, wg_scr[half:, :])
        u = _dot(lo, wu_scr[0:half, :]) + _dot(hi, wu_scr[half:, :])
        act = (g * jax.nn.sigmoid(g) * u).astype(BF16)
        y = _dot(act, wd_scr[...])
        o_ref[...] = _pack_halves(y[:, :half], y[:, half:])

    @pl.when(i >= nu_ref[0])
    def _():
        o_ref[...] = jnp.zeros_like(o_ref)


def _experts(block_e, n_used, xs, w_gate, w_up, w_down):
    n_rows, w = xs.shape
    _, d, f = w_gate.shape
    return pl.pallas_call(
        _experts_kernel,
        grid_spec=pltpu.PrefetchScalarGridSpec(
            num_scalar_prefetch=2,
            grid=(n_rows // ROW_BLOCK,),
            in_specs=[pl.BlockSpec((ROW_BLOCK, w), lambda i, be, nu: (i, 0)),
                      pl.BlockSpec((None, d, f), lambda i, be, nu: (be[i], 0, 0)),
                      pl.BlockSpec((None, d, f), lambda i, be, nu: (be[i], 0, 0)),
                      pl.BlockSpec((None, f, d), lambda i, be, nu: (be[i], 0, 0))],
            out_specs=pl.BlockSpec((ROW_BLOCK, w), lambda i, be, nu: (i, 0)),
            scratch_shapes=[pltpu.VMEM((d, f), BF16), pltpu.VMEM((d, f), BF16), pltpu.VMEM((f, d), BF16)]),
        out_shape=jax.ShapeDtypeStruct((n_rows, w), I32),
        compiler_params=_cparams("arbitrary"),
        name="experts",
    )(block_e, n_used, xs, w_gate, w_up, w_down)


def _final_kernel(base_ref, g_ref, wt_ref, gf_ref, ng_ref, o_ref):
    half = base_ref.shape[1] // 2
    wt = wt_ref[...]
    acc_lo = jnp.zeros((base_ref.shape[0], half), F32)
    acc_hi = jnp.zeros((base_ref.shape[0], half), F32)
    for k in range(TOP_K):
        lo, hi = _unpack_halves(g_ref[k])
        wk = wt[:, k:k + 1]
        acc_lo = acc_lo + wk * lo
        acc_hi = acc_hi + wk * hi
    gf = gf_ref[0]
    x_lo = base_ref[:, :half] + gf[:, :half] * acc_lo
    x_hi = base_ref[:, half:] + gf[:, half:] * acc_hi
    ms = (jnp.sum(x_lo * x_lo, axis=-1, keepdims=True)
          + jnp.sum(x_hi * x_hi, axis=-1, keepdims=True)) * (1.0 / (2 * half))
    inv = lax.rsqrt(ms + EPS)
    o_ref[:, :half] = x_lo * inv * ng_ref[:, :half]
    o_ref[:, half:] = x_hi * inv * ng_ref[:, half:]


def _final(base, gathered, wt, gf, norm_g, seq):
    t, d = base.shape
    tm = 256
    per_seq = seq // tm
    return pl.pallas_call(
        _final_kernel,
        grid=(t // tm,),
        in_specs=[pl.BlockSpec((tm, d), lambda i: (i, 0)),
                  pl.BlockSpec((TOP_K, tm, d // 2), lambda i: (0, i, 0)),
                  pl.BlockSpec((tm, LANES), lambda i: (i, 0)),
                  pl.BlockSpec((1, 1, d), lambda i: (i // per_seq, 0, 0)),
                  pl.BlockSpec((1, d), lambda i: (0, 0))],
        out_specs=pl.BlockSpec((tm, d), lambda i: (i, 0)),
        out_shape=jax.ShapeDtypeStruct((t, d), F32),
        compiler_params=_cparams("arbitrary"),
        name="final",
    )(base, gathered, wt, gf, norm_g.reshape(1, d))


def _block_table(counts, n_blocks):
    padded = (counts + ROW_BLOCK - 1) // ROW_BLOCK * ROW_BLOCK
    pend = jnp.cumsum(padded)
    pstart = (pend - padded).astype(I32)
    n_used = (pend[-1] // ROW_BLOCK).astype(I32)
    blk = jnp.arange(n_blocks, dtype=I32)
    block_e = jnp.minimum(jnp.searchsorted(pend, blk * ROW_BLOCK, side="right"), N_EXPERTS - 1).astype(I32)
    last_e = block_e[jnp.maximum(n_used - 1, 0)]
    block_e = jnp.where(blk < n_used, block_e, last_e)
    return pstart, block_e, n_used.reshape(1)


def _layer(x, c_act, w_ada, b_ada, norm_mix_g, w_in, conv_a_w, conv_b_w, conv_b_b, ln_b_g, ln_b_b,
           head_norm_a_g, head_norm_b_g, w_out, norm_ffn_g, w_router, router_bias, w_gate, w_up, w_down,
           w_shared_gate, w_shared_up, w_shared_down, final_g):
    bsz, seq, d = x.shape
    t = bsz * seq
    xf = x.reshape(t, d)

    mod = _adaln(c_act, w_ada, b_ada)
    sh_m, sc_m, g_m, sh_f, sc_f, g_f = [m.reshape(bsz, 1, d) for m in jnp.split(mod, 6, axis=-1)]

    proj = _inproj(xf, sh_m, sc_m, norm_mix_g.reshape(1, d), w_in.astype(BF16), seq)
    head_g = jnp.concatenate([head_norm_a_g, head_norm_b_g])
    x1 = _mixer(proj, xf, g_m, conv_a_w, conv_b_w, conv_b_b, ln_b_g, ln_b_b, head_g, w_out.astype(BF16),
                bsz, seq)

    base, hp, idx, rank, wt, counts = _ffn_pre(x1, sh_f, sc_f, g_f, norm_ffn_g.reshape(1, d), w_router,
                                               router_bias, w_shared_gate, w_shared_up, w_shared_down, seq)

    n_assign = t * TOP_K
    n_blocks = (n_assign + N_EXPERTS * (ROW_BLOCK - 1) + ROW_BLOCK - 1) // ROW_BLOCK
    pstart, block_e, n_used = _block_table(counts[:, 0], n_blocks)
    dest = _dest(pstart, idx, rank)

    dest3 = dest.reshape(TOP_K, t // SC_CHUNK, SC_CHUNK).transpose(1, 0, 2)
    xs = _dispatch_rows(hp, dest3, n_blocks * ROW_BLOCK)
    ys = _experts(block_e, n_used, xs, w_gate, w_up, w_down)
    gathered = _gather_rows(ys, dest.reshape(-1)).reshape(TOP_K, t, d // 2)
    out = _final(base, gathered, wt, g_f, final_g, seq)
    return out.reshape(bsz, seq, d)


def kernel(x, c, w_ada, b_ada, norm_mix_g, w_in, conv_a_w, conv_b_w, conv_b_b, ln_b_g, ln_b_b, head_norm_a_g,
           head_norm_b_g, w_out, norm_ffn_g, w_router, router_bias, w_gate, w_up, w_down, w_shared_gate,
           w_shared_up, w_shared_down, norm_final_g):
    depth = w_ada.shape[0]
    assert depth == 1, "the fused final norm assumes a single layer"
    return _layer(x, c, w_ada[0], b_ada[0], norm_mix_g[0], w_in[0], conv_a_w[0], conv_b_w[0], conv_b_b[0],
                  ln_b_g[0], ln_b_b[0], head_norm_a_g[0], head_norm_b_g[0], w_out[0], norm_ffn_g[0],
                  w_router[0], router_bias[0], w_gate[0], w_up[0], w_down[0], w_shared_gate[0],
                  w_shared_up[0], w_shared_down[0], norm_final_g)
```

```python
import functools

import jax
import jax.numpy as jnp
from jax import lax
from jax.experimental import pallas as pl
from jax.experimental.pallas import tpu as pltpu
from jax.experimental.pallas import tpu_sc as plsc

F32 = jnp.float32
BF16 = jnp.bfloat16
I32 = jnp.int32

HEAD_CH = 64
K_SHORT = 3
K_CONF = 31
N_EXPERTS = 64
TOP_K = 8
N_GROUPS = 8
TOPK_GROUPS = 4
GROUP_SIZE = N_EXPERTS // N_GROUPS
ROUTED_SCALE = 2.5
EPS = 1e-6

LANES = 128
HALO = 32
ROW_BLOCK = 512
VMEM_LIMIT = 56 * 1024 * 1024
HI_MASK = -65536
SC_CHUNK = 32


def _cparams(*sem):
    return pltpu.CompilerParams(dimension_semantics=sem, vmem_limit_bytes=VMEM_LIMIT)


def _dot(a, b):
    return jnp.dot(a, b, preferred_element_type=F32)


def _split_bf16(x):
    hi = x.astype(BF16)
    lo = (x - hi.astype(F32)).astype(BF16)
    return hi, lo


def _pack_halves(lo, hi):
    lo_b = lax.bitcast_convert_type(lo.astype(BF16).astype(F32), I32)
    hi_b = lax.bitcast_convert_type(hi.astype(BF16).astype(F32), I32)
    return lax.shift_right_logical(lo_b, 16) | (hi_b & HI_MASK)


def _unpack_halves(p):
    lo = lax.bitcast_convert_type(lax.shift_left(p, 16), F32)
    hi = lax.bitcast_convert_type(p & HI_MASK, F32)
    return lo, hi


def _adaln_kernel(c_ref, w_ref, b_ref, o_ref):
    c = c_ref[...]
    ca = c * jax.nn.sigmoid(c)
    chi, clo = _split_bf16(ca)
    whi, wlo = _split_bf16(w_ref[...])
    o_ref[...] = _dot(chi, whi) + _dot(clo, whi) + _dot(chi, wlo) + b_ref[...]


def _adaln(c, w, b):
    bsz, d = c.shape
    n = w.shape[1]
    tn = 1024
    return pl.pallas_call(
        _adaln_kernel,
        grid=(n // tn,),
        in_specs=[pl.BlockSpec((bsz, d), lambda j: (0, 0)),
                  pl.BlockSpec((d, tn), lambda j: (0, j)),
                  pl.BlockSpec((1, tn), lambda j: (0, j))],
        out_specs=pl.BlockSpec((bsz, tn), lambda j: (0, j)),
        out_shape=jax.ShapeDtypeStruct((bsz, n), F32),
        compiler_params=_cparams("arbitrary"),
        name="adaln",
    )(c, w, b.reshape(1, n))


def _inproj_kernel(x_ref, sh_ref, sc_ref, g_ref, w_ref, o_ref, h_scr):
    @pl.when(pl.program_id(1) == 0)
    def _():
        x = x_ref[...]
        ms = jnp.mean(x * x, axis=-1, keepdims=True)
        y = x * lax.rsqrt(ms + EPS) * g_ref[...]
        h_scr[...] = (y * (1.0 + sc_ref[0]) + sh_ref[0]).astype(BF16)

    o_ref[...] = _dot(h_scr[...], w_ref[...]).astype(BF16)


def _inproj(xf, sh, sc, g, w_bf, seq):
    t, d = xf.shape
    n = w_bf.shape[1]
    tm, tn = 1024, 1024
    per_seq = seq // tm
    vec = pl.BlockSpec((1, 1, d), lambda i, j: (i // per_seq, 0, 0))
    return pl.pallas_call(
        _inproj_kernel,
        grid=(t // tm, n // tn),
        in_specs=[pl.BlockSpec((tm, d), lambda i, j: (i, 0)), vec, vec,
                  pl.BlockSpec((1, d), lambda i, j: (0, 0)),
                  pl.BlockSpec((d, tn), lambda i, j: (0, j))],
        out_specs=pl.BlockSpec((tm, tn), lambda i, j: (i, j)),
        out_shape=jax.ShapeDtypeStruct((t, n), BF16),
        scratch_shapes=[pltpu.VMEM((tm, d), BF16)],
        compiler_params=_cparams("arbitrary", "arbitrary"),
        name="inproj",
    )(xf, sh, sc, g, w_bf)


def _mixer_kernel(p_ref, halo_ref, x_ref, gm_ref, wa_ref, wb_ref, bb_ref, lng_ref, lnb_ref, hg_ref,
                  grp_ref, grpt_ref, wout_ref, o_ref, ua_scr, ub_scr):
    ts = p_ref.shape[0]
    da = wa_ref.shape[1]
    keep = (pl.program_id(1) > 0).astype(F32)

    def seg(ref, k):
        return ref[:, k * da:(k + 1) * da].astype(F32)

    ua_scr[0:HALO, :] = seg(halo_ref, 2) * seg(halo_ref, 0) * keep
    ua_scr[HALO:HALO + ts, :] = seg(p_ref, 2) * seg(p_ref, 0)
    ub_scr[0:HALO, :] = seg(halo_ref, 3) * jax.nn.sigmoid(seg(halo_ref, 4)) * keep
    ub_scr[HALO:HALO + ts, :] = seg(p_ref, 3) * jax.nn.sigmoid(seg(p_ref, 4))

    conv_a = jnp.zeros((ts, da), F32)
    for k in range(K_SHORT):
        off = HALO - (K_SHORT - 1) + k
        conv_a = conv_a + wa_ref[k:k + 1, :] * ua_scr[off:off + ts, :]
    ya = seg(p_ref, 1) * conv_a

    zb = jnp.zeros((ts, da), F32) + bb_ref[...]
    for k in range(K_CONF):
        off = HALO - (K_CONF - 1) + k
        zb = zb + wb_ref[k:k + 1, :] * ub_scr[off:off + ts, :]
    mu = jnp.mean(zb, axis=-1, keepdims=True)
    zc = zb - mu
    var = jnp.mean(zc * zc, axis=-1, keepdims=True)
    zn = zc * lax.rsqrt(var + EPS) * lng_ref[...] + lnb_ref[...]
    zs = zn * jax.nn.sigmoid(zn)

    y = jnp.concatenate([ya, zs], axis=-1)
    gsum = _dot((y * y).astype(BF16), grp_ref[...])
    scale = lax.rsqrt(gsum * (1.0 / HEAD_CH) + EPS)
    s_hi, s_lo = _split_bf16(scale)
    scale_full = _dot(s_hi, grpt_ref[...]) + _dot(s_lo, grpt_ref[...])
    yn = (y * scale_full * hg_ref[...]).astype(BF16)
    o_ref[...] = x_ref[...] + gm_ref[0] * _dot(yn, wout_ref[...])


def _mixer(proj, xf, g_m, conv_a_w, conv_b_w, conv_b_b, ln_g, ln_b, head_g, w_out_bf, bsz, seq):
    t, d = xf.shape
    n_in = proj.shape[1]
    da = conv_a_w.shape[1]
    ts = 256
    per_seq = seq // ts
    halo_per_tile = ts // HALO
    n_heads = d // HEAD_CH
    head_of = jnp.arange(d, dtype=I32) // HEAD_CH
    grp = (head_of[:, None] == jnp.arange(n_heads, dtype=I32)[None, :]).astype(BF16)
    const = lambda shape: pl.BlockSpec(shape, lambda b, s: (0,) * len(shape))
    return pl.pallas_call(
        _mixer_kernel,
        grid=(bsz, per_seq),
        in_specs=[
            pl.BlockSpec((ts, n_in), lambda b, s: (b * per_seq + s, 0)),
            pl.BlockSpec((HALO, n_in), lambda b, s: (jnp.maximum((b * per_seq + s) * halo_per_tile - 1, 0), 0)),
            pl.BlockSpec((ts, d), lambda b, s: (b * per_seq + s, 0)),
            pl.BlockSpec((1, 1, d), lambda b, s: (b, 0, 0)),
            const((K_SHORT, da)), const((K_CONF, da)), const((1, da)), const((1, da)), const((1, da)),
            const((1, d)), const((d, n_heads)), const((n_heads, d)), const((d, d)),
        ],
        out_specs=pl.BlockSpec((ts, d), lambda b, s: (b * per_seq + s, 0)),
        out_shape=jax.ShapeDtypeStruct((t, d), F32),
        scratch_shapes=[pltpu.VMEM((HALO + ts, da), F32), pltpu.VMEM((HALO + ts, da), F32)],
        compiler_params=_cparams("arbitrary", "arbitrary"),
        name="mixer",
    )(proj, proj, xf, g_m, conv_a_w, conv_b_w, conv_b_b.reshape(1, da), ln_g.reshape(1, da),
      ln_b.reshape(1, da), head_g.reshape(1, d), grp, grp.T, w_out_bf)


def _first_argmax(vals, iota, size):
    m = jnp.max(vals, axis=0, keepdims=True)
    idx = jnp.min(jnp.where(vals == m, iota, size), axis=0, keepdims=True)
    return m, idx


def _ffn_pre_kernel(x_ref, sh_ref, sc_ref, gf_ref, g_ref, wrh_ref, wrl_ref, rb_ref, tri_ref,
                    wsg_ref, wsu_ref, wsd_ref,
                    base_ref, hp_ref, idx_ref, rank_ref, wt_ref, cnt_ref, carry_scr):
    tm, d = x_ref.shape
    half = d // 2

    @pl.when(pl.program_id(0) == 0)
    def _():
        carry_scr[...] = jnp.zeros_like(carry_scr)

    x = x_ref[...]
    ms = jnp.mean(x * x, axis=-1, keepdims=True)
    h = x * lax.rsqrt(ms + EPS) * g_ref[...] * (1.0 + sc_ref[0]) + sh_ref[0]
    h_hi, h_lo = _split_bf16(h)
    hp_ref[...] = _pack_halves(h[:, :half], h[:, half:])

    nt = (((1,), (1,)), ((), ()))
    logits = (lax.dot_general(wrh_ref[...], h_hi, nt, preferred_element_type=F32)
              + lax.dot_general(wrh_ref[...], h_lo, nt, preferred_element_type=F32)
              + lax.dot_general(wrl_ref[...], h_hi, nt, preferred_element_type=F32))
    scores = jax.nn.sigmoid(logits)
    biased = scores + rb_ref[...]

    b3 = biased.reshape(N_GROUPS, GROUP_SIZE, tm)
    sub_iota = lax.broadcasted_iota(I32, (N_GROUPS, GROUP_SIZE, tm), 1)
    m1 = jnp.max(b3, axis=1, keepdims=True)
    i1 = jnp.min(jnp.where(b3 == m1, sub_iota, GROUP_SIZE), axis=1, keepdims=True)
    m2 = jnp.max(jnp.where(sub_iota == i1, -jnp.inf, b3), axis=1, keepdims=True)
    gscore = (m1 + m2).reshape(N_GROUPS, tm)

    g_iota = lax.broadcasted_iota(I32, (N_GROUPS, tm), 0)
    gsel = jnp.zeros((N_GROUPS, tm), jnp.bool_)
    gwork = gscore
    for _ in range(TOPK_GROUPS):
        _, gi = _first_argmax(gwork, g_iota, N_GROUPS)
        hit = g_iota == gi
        gsel = gsel | hit
        gwork = jnp.where(hit, -jnp.inf, gwork)
    emask = jnp.broadcast_to(gsel.reshape(N_GROUPS, 1, tm), (N_GROUPS, GROUP_SIZE, tm)).reshape(N_EXPERTS, tm)

    e_iota = lax.broadcasted_iota(I32, (N_EXPERTS, tm), 0)
    work = jnp.where(emask, biased, -jnp.inf)
    chosen = jnp.zeros((N_EXPERTS, tm), jnp.bool_)
    idx_rows, w_rows = [], []
    for _ in range(TOP_K):
        _, ei = _first_argmax(work, e_iota, N_EXPERTS)
        hit = e_iota == ei
        chosen = chosen | hit
        work = jnp.where(hit, -jnp.inf, work)
        idx_rows.append(ei)
        w_rows.append(jnp.sum(jnp.where(hit, scores, 0.0), axis=0, keepdims=True))
    idx = jnp.concatenate(idx_rows, axis=0)
    w = jnp.concatenate(w_rows, axis=0)
    w = w / jnp.sum(w, axis=0, keepdims=True) * ROUTED_SCALE

    onehot = chosen.astype(BF16)
    before = _dot(onehot, tri_ref[...]) + carry_scr[:, 0:1]
    rank_rows = [jnp.sum(jnp.where(e_iota == idx_rows[k], before, 0.0), axis=0, keepdims=True)
                 for k in range(TOP_K)]
    carry_new = carry_scr[...] + jnp.sum(chosen.astype(F32), axis=1, keepdims=True)
    carry_scr[...] = carry_new
    cnt_ref[...] = carry_new.astype(I32)
    idx_ref[...] = idx
    rank_ref[...] = jnp.concatenate(rank_rows, axis=0).astype(I32)
    w_pad = jnp.concatenate([w, jnp.zeros((LANES - TOP_K, tm), F32)], axis=0)
    wt_ref[...] = w_pad.T

    hb = h_hi
    act = jax.nn.silu(_dot(hb, wsg_ref[...])) * _dot(hb, wsu_ref[...])
    base_ref[...] = x + gf_ref[0] * _dot(act.astype(BF16), wsd_ref[...])


def _ffn_pre(x1, sh, sc, gf, g, w_router, router_bias, wsg, wsu, wsd, seq):
    t, d = x1.shape
    f = wsg.shape[1]
    tm = 512
    per_seq = seq // tm
    wr_hi, wr_lo = _split_bf16(w_router.T)
    tri = (jnp.arange(tm, dtype=I32)[:, None] < jnp.arange(tm, dtype=I32)[None, :]).astype(BF16)
    vec = pl.BlockSpec((1, 1, d), lambda i: (i // per_seq, 0, 0))
    const = lambda shape: pl.BlockSpec(shape, lambda i: (0,) * len(shape))
    return pl.pallas_call(
        _ffn_pre_kernel,
        grid=(t // tm,),
        in_specs=[pl.BlockSpec((tm, d), lambda i: (i, 0)), vec, vec, vec, const((1, d)),
                  const((N_EXPERTS, d)), const((N_EXPERTS, d)), const((N_EXPERTS, 1)), const((tm, tm)),
                  const((d, f)), const((d, f)), const((f, d))],
        out_specs=[pl.BlockSpec((tm, d), lambda i: (i, 0)),
                   pl.BlockSpec((tm, d // 2), lambda i: (i, 0)),
                   pl.BlockSpec((TOP_K, tm), lambda i: (0, i)),
                   pl.BlockSpec((TOP_K, tm), lambda i: (0, i)),
                   pl.BlockSpec((tm, LANES), lambda i: (i, 0)),
                   const((N_EXPERTS, LANES))],
        out_shape=[jax.ShapeDtypeStruct((t, d), F32),
                   jax.ShapeDtypeStruct((t, d // 2), I32),
                   jax.ShapeDtypeStruct((TOP_K, t), I32),
                   jax.ShapeDtypeStruct((TOP_K, t), I32),
                   jax.ShapeDtypeStruct((t, LANES), F32),
                   jax.ShapeDtypeStruct((N_EXPERTS, LANES), I32)],
        scratch_shapes=[pltpu.VMEM((N_EXPERTS, LANES), F32)],
        compiler_params=_cparams("arbitrary"),
        name="ffn_pre",
    )(x1, sh, sc, gf, g, wr_hi, wr_lo, router_bias.reshape(N_EXPERTS, 1), tri,
      wsg.astype(BF16), wsu.astype(BF16), wsd.astype(BF16))


def _dest_kernel(pstart_ref, idx_ref, rank_ref, o_ref):
    idx = idx_ref[...]
    acc = rank_ref[...]
    for e in range(N_EXPERTS):
        acc = acc + jnp.where(idx == e, pstart_ref[e], 0)
    o_ref[...] = acc


def _dest(pstart, idx, rank):
    k, t = idx.shape
    tl = min(4096, t)
    return pl.pallas_call(
        _dest_kernel,
        grid_spec=pltpu.PrefetchScalarGridSpec(
            num_scalar_prefetch=1,
            grid=(t // tl,),
            in_specs=[pl.BlockSpec((k, tl), lambda i, ps: (0, i)),
                      pl.BlockSpec((k, tl), lambda i, ps: (0, i))],
            out_specs=pl.BlockSpec((k, tl), lambda i, ps: (0, i))),
        out_shape=jax.ShapeDtypeStruct((k, t), I32),
        compiler_params=_cparams("arbitrary"),
        name="dest",
    )(pstart, idx, rank)


def _sc_mesh():
    return plsc.VectorSubcoreMesh(core_axis_name="c", subcore_axis_name="s")


def _sc_workers():
    info = plsc.get_sparse_core_info()
    return info.num_cores, info.num_cores * info.num_subcores


def _dispatch_rows(hp, dest3, n_rows):
    t, w = hp.shape
    n_chunks, top_k, chunk = dest3.shape
    n_cores, n_workers = _sc_workers()
    per_worker = n_chunks // n_workers

    @functools.partial(
        pl.kernel, mesh=_sc_mesh(),
        out_type=jax.ShapeDtypeStruct((n_rows, w), hp.dtype),
        scratch_types=[pltpu.VMEM((top_k, chunk), I32), pltpu.VMEM((chunk, w), hp.dtype),
                       pltpu.SemaphoreType.DMA],
        name="dispatch_rows")
    def k(hp_hbm, dest_hbm, xs_hbm, idx_v, rows_v, sem):
        wid = lax.axis_index("s") * n_cores + lax.axis_index("c")

        @pl.loop(0, per_worker)
        def _(j):
            c = wid * per_worker + j
            pltpu.sync_copy(dest_hbm.at[c], idx_v)
            pltpu.sync_copy(hp_hbm.at[pl.ds(c * chunk, chunk)], rows_v)
            copies = [pltpu.async_copy(rows_v, xs_hbm.at[idx_v.at[q]], sem) for q in range(top_k)]
            for cp in copies:
                cp.wait()

    return k(hp, dest3)


def _gather_rows(ys, flat_idx):
    n = flat_idx.shape[0]
    w = ys.shape[1]
    n_cores, n_workers = _sc_workers()
    per_worker = n // n_workers
    steps = per_worker // SC_CHUNK

    @functools.partial(
        pl.kernel, mesh=_sc_mesh(),
        out_type=jax.ShapeDtypeStruct((n, w), ys.dtype),
        scratch_types=[pltpu.VMEM((SC_CHUNK,), I32), pltpu.VMEM((SC_CHUNK, w), ys.dtype),
                       pltpu.SemaphoreType.DMA],
        name="gather_rows")
    def k(ys_hbm, idx_hbm, out_hbm, idx_v, rows_v, sem):
        wid = lax.axis_index("s") * n_cores + lax.axis_index("c")

        @pl.loop(0, steps)
        def _(j):
            base = wid * per_worker + j * SC_CHUNK
            pltpu.sync_copy(idx_hbm.at[pl.ds(base, SC_CHUNK)], idx_v)
            pltpu.async_copy(ys_hbm.at[idx_v], rows_v, sem).wait()
            pltpu.sync_copy(rows_v, out_hbm.at[pl.ds(base, SC_CHUNK)])

    return k(ys, flat_idx)


def _experts_kernel(be_ref, nu_ref, x_ref, wg_ref, wu_ref, wd_ref, o_ref, wg_scr, wu_scr, wd_scr):
    i = pl.program_id(0)
    prev = be_ref[jnp.maximum(i - 1, 0)]
    fresh = jnp.logical_or(i == 0, be_ref[i] != prev)

    @pl.when(fresh)
    def _():
        wg_scr[...] = wg_ref[...].astype(BF16)
        wu_scr[...] = wu_ref[...].astype(BF16)
        wd_scr[...] = wd_ref[...].astype(BF16)

    @pl.when(i < nu_ref[0])
    def _():
        half = wg_scr.shape[0] // 2
        lo, hi = _unpack_halves(x_ref[...])
        lo = lo.astype(BF16)
        hi = hi.astype(BF16)
        g = _dot(lo, wg_scr[0:half, :]) + _dot(hi, wg_scr[half:, :])
        u = _dot(lo, wu_scr[0:half, :]) + _dot(hi, wu_scr[half:, :])
        act = (g * jax.nn.sigmoid(g) * u).astype(BF16)
        y = _dot(act, wd_scr[...])
        o_ref[...] = _pack_halves(y[:, :half], y[:, half:])

    @pl.when(i >= nu_ref[0])
    def _():
        o_ref[...] = jnp.zeros_like(o_ref)


def _experts(block_e, n_used, xs, w_gate, w_up, w_down):
    n_rows, w = xs.shape
    _, d, f = w_gate.shape
    return pl.pallas_call(
        _experts_kernel,
        grid_spec=pltpu.PrefetchScalarGridSpec(
            num_scalar_prefetch=2,
            grid=(n_rows // ROW_BLOCK,),
            in_specs=[pl.BlockSpec((ROW_BLOCK, w), lambda i, be, nu: (i, 0)),
                      pl.BlockSpec((None, d, f), lambda i, be, nu: (be[i], 0, 0)),
                      pl.BlockSpec((None, d, f), lambda i, be, nu: (be[i], 0, 0)),
                      pl.BlockSpec((None, f, d), lambda i, be, nu: (be[i], 0, 0))],
            out_specs=pl.BlockSpec((ROW_BLOCK, w), lambda i, be, nu: (i, 0)),
            scratch_shapes=[pltpu.VMEM((d, f), BF16), pltpu.VMEM((d, f), BF16), pltpu.VMEM((f, d), BF16)]),
        out_shape=jax.ShapeDtypeStruct((n_rows, w), I32),
        compiler_params=_cparams("arbitrary"),
        name="experts",
    )(block_e, n_used, xs, w_gate, w_up, w_down)


def _final_kernel(base_ref, g_ref, wt_ref, gf_ref, ng_ref, o_ref):
    half = base_ref.shape[1] // 2
    wt = wt_ref[...]
    acc_lo = jnp.zeros((base_ref.shape[0], half), F32)
    acc_hi = jnp.zeros((base_ref.shape[0], half), F32)
    for k in range(TOP_K):
        lo, hi = _unpack_halves(g_ref[k])
        wk = wt[:, k:k + 1]
        acc_lo = acc_lo + wk * lo
        acc_hi = acc_hi + wk * hi
    gf = gf_ref[0]
    x_lo = base_ref[:, :half] + gf[:, :half] * acc_lo
    x_hi = base_ref[:, half:] + gf[:, half:] * acc_hi
    ms = (jnp.sum(x_lo * x_lo, axis=-1, keepdims=True)
          + jnp.sum(x_hi * x_hi, axis=-1, keepdims=True)) * (1.0 / (2 * half))
    inv = lax.rsqrt(ms + EPS)
    o_ref[:, :half] = x_lo * inv * ng_ref[:, :half]
    o_ref[:, half:] = x_hi * inv * ng_ref[:, half:]


def _final(base, gathered, wt, gf, norm_g, seq):
    t, d = base.shape
    tm = 256
    per_seq = seq // tm
    return pl.pallas_call(
        _final_kernel,
        grid=(t // tm,),
        in_specs=[pl.BlockSpec((tm, d), lambda i: (i, 0)),
                  pl.BlockSpec((TOP_K, tm, d // 2), lambda i: (0, i, 0)),
                  pl.BlockSpec((tm, LANES), lambda i: (i, 0)),
                  pl.BlockSpec((1, 1, d), lambda i: (i // per_seq, 0, 0)),
                  pl.BlockSpec((1, d), lambda i: (0, 0))],
        out_specs=pl.BlockSpec((tm, d), lambda i: (i, 0)),
        out_shape=jax.ShapeDtypeStruct((t, d), F32),
        compiler_params=_cparams("arbitrary"),
        name="final",
    )(base, gathered, wt, gf, norm_g.reshape(1, d))


def _block_table(counts, n_blocks):
    padded = (counts + ROW_BLOCK - 1) // ROW_BLOCK * ROW_BLOCK
    pend = jnp.cumsum(padded)
    pstart = (pend - padded).astype(I32)
    n_used = (pend[-1] // ROW_BLOCK).astype(I32)
    blk = jnp.arange(n_blocks, dtype=I32)
    ended = (pend[None, :] <= (blk * ROW_BLOCK)[:, None]).astype(I32)
    block_e = jnp.minimum(jnp.sum(ended, axis=1), N_EXPERTS - 1).astype(I32)
    last_e = block_e[jnp.maximum(n_used - 1, 0)]
    block_e = jnp.where(blk < n_used, block_e, last_e)
    return pstart, block_e, n_used.reshape(1)


def _layer(x, c_act, w_ada, b_ada, norm_mix_g, w_in, conv_a_w, conv_b_w, conv_b_b, ln_b_g, ln_b_b,
           head_norm_a_g, head_norm_b_g, w_out, norm_ffn_g, w_router, router_bias, w_gate, w_up, w_down,
           w_shared_gate, w_shared_up, w_shared_down, final_g):
    bsz, seq, d = x.shape
    t = bsz * seq
    xf = x.reshape(t, d)

    mod = _adaln(c_act, w_ada, b_ada)
    sh_m, sc_m, g_m, sh_f, sc_f, g_f = [m.reshape(bsz, 1, d) for m in jnp.split(mod, 6, axis=-1)]

    proj = _inproj(xf, sh_m, sc_m, norm_mix_g.reshape(1, d), w_in.astype(BF16), seq)
    head_g = jnp.concatenate([head_norm_a_g, head_norm_b_g])
    x1 = _mixer(proj, xf, g_m, conv_a_w, conv_b_w, conv_b_b, ln_b_g, ln_b_b, head_g, w_out.astype(BF16),
                bsz, seq)

    base, hp, idx, rank, wt, counts = _ffn_pre(x1, sh_f, sc_f, g_f, norm_ffn_g.reshape(1, d), w_router,
                                               router_bias, w_shared_gate, w_shared_up, w_shared_down, seq)

    n_assign = t * TOP_K
    n_blocks = (n_assign + N_EXPERTS * (ROW_BLOCK - 1) + ROW_BLOCK - 1) // ROW_BLOCK
    pstart, block_e, n_used = _block_table(counts[:, 0], n_blocks)
    dest = _dest(pstart, idx, rank)

    dest3 = dest.reshape(TOP_K, t // SC_CHUNK, SC_CHUNK).transpose(1, 0, 2)
    xs = _dispatch_rows(hp, dest3, n_blocks * ROW_BLOCK)
    ys = _experts(block_e, n_used, xs, w_gate, w_up, w_down)
    gathered = _gather_rows(ys, dest.reshape(-1)).reshape(TOP_K, t, d // 2)
    out = _final(base, gathered, wt, g_f, final_g, seq)
    return out.reshape(bsz, seq, d)


def kernel(x, c, w_ada, b_ada, norm_mix_g, w_in, conv_a_w, conv_b_w, conv_b_b, ln_b_g, ln_b_b, head_norm_a_g,
           head_norm_b_g, w_out, norm_ffn_g, w_router, router_bias, w_gate, w_up, w_down, w_shared_gate,
           w_shared_up, w_shared_down, norm_final_g):
    depth = w_ada.shape[0]
    assert depth == 1, "the fused final norm assumes a single layer"
    return _layer(x, c, w_ada[0], b_ada[0], norm_mix_g[0], w_in[0], conv_a_w[0], conv_b_w[0], conv_b_b[0],
                  ln_b_g[0], ln_b_b[0], head_norm_a_g[0], head_norm_b_g[0], w_out[0], norm_ffn_g[0],
                  w_router[0], router_bias[0], w_gate[0], w_up[0], w_down[0], w_shared_gate[0],
                  w_shared_up[0], w_shared_down[0], norm_final_g)
```

```python
import functools

import jax
import jax.numpy as jnp
from jax import lax
from jax.experimental import pallas as pl
from jax.experimental.pallas import tpu as pltpu
from jax.experimental.pallas import tpu_sc as plsc

F32 = jnp.float32
BF16 = jnp.bfloat16
I32 = jnp.int32

HEAD_CH = 64
K_SHORT = 3
K_CONF = 31
N_EXPERTS = 64
TOP_K = 8
N_GROUPS = 8
TOPK_GROUPS = 4
GROUP_SIZE = N_EXPERTS // N_GROUPS
ROUTED_SCALE = 2.5
EPS = 1e-6

LANES = 128
HALO = 32
ROW_BLOCK = 512
VMEM_LIMIT = 56 * 1024 * 1024
HI_MASK = -65536
SC_CHUNK = 32


def _cparams(*sem):
    return pltpu.CompilerParams(dimension_semantics=sem, vmem_limit_bytes=VMEM_LIMIT)


def _dot(a, b):
    return jnp.dot(a, b, preferred_element_type=F32)


def _split_bf16(x):
    hi = x.astype(BF16)
    lo = (x - hi.astype(F32)).astype(BF16)
    return hi, lo


def _pack_halves(lo, hi):
    lo_b = lax.bitcast_convert_type(lo.astype(BF16).astype(F32), I32)
    hi_b = lax.bitcast_convert_type(hi.astype(BF16).astype(F32), I32)
    return lax.shift_right_logical(lo_b, 16) | (hi_b & HI_MASK)


def _unpack_halves(p):
    lo = lax.bitcast_convert_type(lax.shift_left(p, 16), F32)
    hi = lax.bitcast_convert_type(p & HI_MASK, F32)
    return lo, hi


def _adaln_kernel(c_ref, w_ref, b_ref, o_ref):
    c = c_ref[...]
    ca = c * jax.nn.sigmoid(c)
    chi, clo = _split_bf16(ca)
    whi, wlo = _split_bf16(w_ref[...])
    o_ref[...] = _dot(chi, whi) + _dot(clo, whi) + _dot(chi, wlo) + b_ref[...]


def _adaln(c, w, b):
    bsz, d = c.shape
    n = w.shape[1]
    tn = 1024
    return pl.pallas_call(
        _adaln_kernel,
        grid=(n // tn,),
        in_specs=[pl.BlockSpec((bsz, d), lambda j: (0, 0)),
                  pl.BlockSpec((d, tn), lambda j: (0, j)),
                  pl.BlockSpec((1, tn), lambda j: (0, j))],
        out_specs=pl.BlockSpec((bsz, tn), lambda j: (0, j)),
        out_shape=jax.ShapeDtypeStruct((bsz, n), F32),
        compiler_params=_cparams("arbitrary"),
        name="adaln",
    )(c, w, b.reshape(1, n))


def _inproj_kernel(x_ref, sh_ref, sc_ref, g_ref, w_ref, o_ref, h_scr):
    @pl.when(pl.program_id(1) == 0)
    def _():
        x = x_ref[...]
        ms = jnp.mean(x * x, axis=-1, keepdims=True)
        y = x * lax.rsqrt(ms + EPS) * g_ref[...]
        h_scr[...] = (y * (1.0 + sc_ref[0]) + sh_ref[0]).astype(BF16)

    o_ref[...] = _dot(h_scr[...], w_ref[...]).astype(BF16)


def _inproj(xf, sh, sc, g, w_bf, seq, b0, nb):
    d = xf.shape[1]
    t = nb * seq
    n = w_bf.shape[1]
    tm, tn = 1024, 1024
    per_seq = seq // tm
    vec = pl.BlockSpec((1, 1, d), lambda i, j: (b0 + i // per_seq, 0, 0))
    return pl.pallas_call(
        _inproj_kernel,
        grid=(t // tm, n // tn),
        in_specs=[pl.BlockSpec((tm, d), lambda i, j: (b0 * per_seq + i, 0)), vec, vec,
                  pl.BlockSpec((1, d), lambda i, j: (0, 0)),
                  pl.BlockSpec((d, tn), lambda i, j: (0, j))],
        out_specs=pl.BlockSpec((tm, tn), lambda i, j: (i, j)),
        out_shape=jax.ShapeDtypeStruct((t, n), BF16),
        scratch_shapes=[pltpu.VMEM((tm, d), BF16)],
        compiler_params=_cparams("arbitrary", "arbitrary"),
        name="inproj",
    )(xf, sh, sc, g, w_bf)


def _mixer_kernel(p_ref, halo_ref, x_ref, gm_ref, wa_ref, wb_ref, bb_ref, lng_ref, lnb_ref, hg_ref,
                  grp_ref, grpt_ref, wout_ref, o_ref, ua_scr, ub_scr):
    ts = p_ref.shape[0]
    da = wa_ref.shape[1]
    keep = (pl.program_id(1) > 0).astype(F32)

    def seg(ref, k):
        return ref[:, k * da:(k + 1) * da].astype(F32)

    ua_scr[0:HALO, :] = seg(halo_ref, 2) * seg(halo_ref, 0) * keep
    ua_scr[HALO:HALO + ts, :] = seg(p_ref, 2) * seg(p_ref, 0)
    ub_scr[0:HALO, :] = seg(halo_ref, 3) * jax.nn.sigmoid(seg(halo_ref, 4)) * keep
    ub_scr[HALO:HALO + ts, :] = seg(p_ref, 3) * jax.nn.sigmoid(seg(p_ref, 4))

    conv_a = jnp.zeros((ts, da), F32)
    for k in range(K_SHORT):
        off = HALO - (K_SHORT - 1) + k
        conv_a = conv_a + wa_ref[k:k + 1, :] * ua_scr[off:off + ts, :]
    ya = seg(p_ref, 1) * conv_a

    zb = jnp.zeros((ts, da), F32) + bb_ref[...]
    for k in range(K_CONF):
        off = HALO - (K_CONF - 1) + k
        zb = zb + wb_ref[k:k + 1, :] * ub_scr[off:off + ts, :]
    mu = jnp.mean(zb, axis=-1, keepdims=True)
    zc = zb - mu
    var = jnp.mean(zc * zc, axis=-1, keepdims=True)
    zn = zc * lax.rsqrt(var + EPS) * lng_ref[...] + lnb_ref[...]
    zs = zn * jax.nn.sigmoid(zn)

    y = jnp.concatenate([ya, zs], axis=-1)
    gsum = _dot((y * y).astype(BF16), grp_ref[...])
    scale = lax.rsqrt(gsum * (1.0 / HEAD_CH) + EPS)
    s_hi, s_lo = _split_bf16(scale)
    scale_full = _dot(s_hi, grpt_ref[...]) + _dot(s_lo, grpt_ref[...])
    yn = (y * scale_full * hg_ref[...]).astype(BF16)
    o_ref[...] = x_ref[...] + gm_ref[0] * _dot(yn, wout_ref[...])


def _mixer(proj, xf, g_m, conv_a_w, conv_b_w, conv_b_b, ln_g, ln_b, head_g, w_out_bf, seq, b0, nb):
    d = xf.shape[1]
    t = nb * seq
    n_in = proj.shape[1]
    da = conv_a_w.shape[1]
    ts = 256
    per_seq = seq // ts
    halo_per_tile = ts // HALO
    n_heads = d // HEAD_CH
    head_of = jnp.arange(d, dtype=I32) // HEAD_CH
    grp = (head_of[:, None] == jnp.arange(n_heads, dtype=I32)[None, :]).astype(BF16)
    const = lambda shape: pl.BlockSpec(shape, lambda b, s: (0,) * len(shape))
    return pl.pallas_call(
        _mixer_kernel,
        grid=(nb, per_seq),
        in_specs=[
            pl.BlockSpec((ts, n_in), lambda b, s: (b * per_seq + s, 0)),
            pl.BlockSpec((HALO, n_in), lambda b, s: (jnp.maximum((b * per_seq + s) * halo_per_tile - 1, 0), 0)),
            pl.BlockSpec((ts, d), lambda b, s: ((b0 + b) * per_seq + s, 0)),
            pl.BlockSpec((1, 1, d), lambda b, s: (b0 + b, 0, 0)),
            const((K_SHORT, da)), const((K_CONF, da)), const((1, da)), const((1, da)), const((1, da)),
            const((1, d)), const((d, n_heads)), const((n_heads, d)), const((d, d)),
        ],
        out_specs=pl.BlockSpec((ts, d), lambda b, s: (b * per_seq + s, 0)),
        out_shape=jax.ShapeDtypeStruct((t, d), F32),
        scratch_shapes=[pltpu.VMEM((HALO + ts, da), F32), pltpu.VMEM((HALO + ts, da), F32)],
        compiler_params=_cparams("arbitrary", "arbitrary"),
        name="mixer",
    )(proj, proj, xf, g_m, conv_a_w, conv_b_w, conv_b_b.reshape(1, da), ln_g.reshape(1, da),
      ln_b.reshape(1, da), head_g.reshape(1, d), grp, grp.T, w_out_bf)


def _first_argmax(vals, iota, size):
    m = jnp.max(vals, axis=0, keepdims=True)
    idx = jnp.min(jnp.where(vals == m, iota, size), axis=0, keepdims=True)
    return m, idx


def _ffn_pre_kernel(x_ref, sh_ref, sc_ref, gf_ref, g_ref, wrh_ref, wrl_ref, rb_ref, tri_ref,
                    wsg_ref, wsu_ref, wsd_ref,
                    base_ref, hp_ref, idx_ref, rank_ref, wt_ref, cnt_ref, carry_scr):
    tm, d = x_ref.shape
    half = d // 2

    @pl.when(pl.program_id(0) == 0)
    def _():
        carry_scr[...] = jnp.zeros_like(carry_scr)

    x = x_ref[...]
    ms = jnp.mean(x * x, axis=-1, keepdims=True)
    h = x * lax.rsqrt(ms + EPS) * g_ref[...] * (1.0 + sc_ref[0]) + sh_ref[0]
    h_hi, h_lo = _split_bf16(h)
    hp_ref[...] = _pack_halves(h[:, :half], h[:, half:])

    nt = (((1,), (1,)), ((), ()))
    logits = (lax.dot_general(wrh_ref[...], h_hi, nt, preferred_element_type=F32)
              + lax.dot_general(wrh_ref[...], h_lo, nt, preferred_element_type=F32)
              + lax.dot_general(wrl_ref[...], h_hi, nt, preferred_element_type=F32))
    scores = jax.nn.sigmoid(logits)
    biased = scores + rb_ref[...]

    b3 = biased.reshape(N_GROUPS, GROUP_SIZE, tm)
    sub_iota = lax.broadcasted_iota(I32, (N_GROUPS, GROUP_SIZE, tm), 1)
    m1 = jnp.max(b3, axis=1, keepdims=True)
    i1 = jnp.min(jnp.where(b3 == m1, sub_iota, GROUP_SIZE), axis=1, keepdims=True)
    m2 = jnp.max(jnp.where(sub_iota == i1, -jnp.inf, b3), axis=1, keepdims=True)
    gscore = (m1 + m2).reshape(N_GROUPS, tm)

    g_iota = lax.broadcasted_iota(I32, (N_GROUPS, tm), 0)
    gsel = jnp.zeros((N_GROUPS, tm), jnp.bool_)
    gwork = gscore
    for _ in range(TOPK_GROUPS):
        _, gi = _first_argmax(gwork, g_iota, N_GROUPS)
        hit = g_iota == gi
        gsel = gsel | hit
        gwork = jnp.where(hit, -jnp.inf, gwork)
    emask = jnp.broadcast_to(gsel.reshape(N_GROUPS, 1, tm), (N_GROUPS, GROUP_SIZE, tm)).reshape(N_EXPERTS, tm)

    e_iota = lax.broadcasted_iota(I32, (N_EXPERTS, tm), 0)
    work = jnp.where(emask, biased, -jnp.inf)
    chosen = jnp.zeros((N_EXPERTS, tm), jnp.bool_)
    idx_rows, w_rows = [], []
    for _ in range(TOP_K):
        _, ei = _first_argmax(work, e_iota, N_EXPERTS)
        hit = e_iota == ei
        chosen = chosen | hit
        work = jnp.where(hit, -jnp.inf, work)
        idx_rows.append(ei)
        w_rows.append(jnp.sum(jnp.where(hit, scores, 0.0), axis=0, keepdims=True))
    idx = jnp.concatenate(idx_rows, axis=0)
    w = jnp.concatenate(w_rows, axis=0)
    w = w / jnp.sum(w, axis=0, keepdims=True) * ROUTED_SCALE

    onehot = chosen.astype(BF16)
    before = _dot(onehot, tri_ref[...]) + carry_scr[:, 0:1]
    rank_rows = [jnp.sum(jnp.where(e_iota == idx_rows[k], before, 0.0), axis=0, keepdims=True)
                 for k in range(TOP_K)]
    carry_new = carry_scr[...] + jnp.sum(chosen.astype(F32), axis=1, keepdims=True)
    carry_scr[...] = carry_new
    cnt_ref[...] = carry_new.astype(I32)
    idx_ref[...] = idx
    rank_ref[...] = jnp.concatenate(rank_rows, axis=0).astype(I32)
    w_pad = jnp.concatenate([w, jnp.zeros((LANES - TOP_K, tm), F32)], axis=0)
    wt_ref[...] = w_pad.T

    hb = h_hi
    act = jax.nn.silu(_dot(hb, wsg_ref[...])) * _dot(hb, wsu_ref[...])
    base_ref[...] = x + gf_ref[0] * _dot(act.astype(BF16), wsd_ref[...])


def _ffn_pre(x1, sh, sc, gf, g, w_router, router_bias, wsg, wsu, wsd, seq, b0):
    t, d = x1.shape
    f = wsg.shape[1]
    tm = 512
    per_seq = seq // tm
    wr_hi, wr_lo = _split_bf16(w_router.T)
    tri = (jnp.arange(tm, dtype=I32)[:, None] < jnp.arange(tm, dtype=I32)[None, :]).astype(BF16)
    vec = pl.BlockSpec((1, 1, d), lambda i: (b0 + i // per_seq, 0, 0))
    const = lambda shape: pl.BlockSpec(shape, lambda i: (0,) * len(shape))
    return pl.pallas_call(
        _ffn_pre_kernel,
        grid=(t // tm,),
        in_specs=[pl.BlockSpec((tm, d), lambda i: (i, 0)), vec, vec, vec, const((1, d)),
                  const((N_EXPERTS, d)), const((N_EXPERTS, d)), const((N_EXPERTS, 1)), const((tm, tm)),
                  const((d, f)), const((d, f)), const((f, d))],
        out_specs=[pl.BlockSpec((tm, d), lambda i: (i, 0)),
                   pl.BlockSpec((tm, d // 2), lambda i: (i, 0)),
                   pl.BlockSpec((TOP_K, tm), lambda i: (0, i)),
                   pl.BlockSpec((TOP_K, tm), lambda i: (0, i)),
                   pl.BlockSpec((tm, LANES), lambda i: (i, 0)),
                   const((N_EXPERTS, LANES))],
        out_shape=[jax.ShapeDtypeStruct((t, d), F32),
                   jax.ShapeDtypeStruct((t, d // 2), I32),
                   jax.ShapeDtypeStruct((TOP_K, t), I32),
                   jax.ShapeDtypeStruct((TOP_K, t), I32),
                   jax.ShapeDtypeStruct((t, LANES), F32),
                   jax.ShapeDtypeStruct((N_EXPERTS, LANES), I32)],
        scratch_shapes=[pltpu.VMEM((N_EXPERTS, LANES), F32)],
        compiler_params=_cparams("arbitrary"),
        name="ffn_pre",
    )(x1, sh, sc, gf, g, wr_hi, wr_lo, router_bias.reshape(N_EXPERTS, 1), tri,
      wsg.astype(BF16), wsu.astype(BF16), wsd.astype(BF16))


def _dest_kernel(pstart_ref, idx_ref, rank_ref, o_ref):
    idx = idx_ref[...]
    acc = rank_ref[...]
    for e in range(N_EXPERTS):
        acc = acc + jnp.where(idx == e, pstart_ref[e], 0)
    o_ref[...] = acc


def _dest(pstart, idx, rank):
    k, t = idx.shape
    tl = min(4096, t)
    return pl.pallas_call(
        _dest_kernel,
        grid_spec=pltpu.PrefetchScalarGridSpec(
            num_scalar_prefetch=1,
            grid=(t // tl,),
            in_specs=[pl.BlockSpec((k, tl), lambda i, ps: (0, i)),
                      pl.BlockSpec((k, tl), lambda i, ps: (0, i))],
            out_specs=pl.BlockSpec((k, tl), lambda i, ps: (0, i))),
        out_shape=jax.ShapeDtypeStruct((k, t), I32),
        compiler_params=_cparams("arbitrary"),
        name="dest",
    )(pstart, idx, rank)


def _sc_mesh():
    return plsc.VectorSubcoreMesh(core_axis_name="c", subcore_axis_name="s")


def _sc_workers():
    info = plsc.get_sparse_core_info()
    return info.num_cores, info.num_cores * info.num_subcores


def _dispatch_rows(hp, dest3, n_rows):
    t, w = hp.shape
    n_chunks, top_k, chunk = dest3.shape
    n_cores, n_workers = _sc_workers()
    per_worker = n_chunks // n_workers

    @functools.partial(
        pl.kernel, mesh=_sc_mesh(),
        out_type=jax.ShapeDtypeStruct((n_rows, w), hp.dtype),
        scratch_types=[pltpu.VMEM((top_k, chunk), I32), pltpu.VMEM((chunk, w), hp.dtype),
                       pltpu.SemaphoreType.DMA],
        name="dispatch_rows")
    def k(hp_hbm, dest_hbm, xs_hbm, idx_v, rows_v, sem):
        wid = lax.axis_index("s") * n_cores + lax.axis_index("c")

        @pl.loop(0, per_worker)
        def _(j):
            c = wid * per_worker + j
            pltpu.sync_copy(dest_hbm.at[c], idx_v)
            pltpu.sync_copy(hp_hbm.at[pl.ds(c * chunk, chunk)], rows_v)
            copies = [pltpu.async_copy(rows_v, xs_hbm.at[idx_v.at[q]], sem) for q in range(top_k)]
            for cp in copies:
                cp.wait()

    return k(hp, dest3)


def _gather_rows(ys, flat_idx):
    n = flat_idx.shape[0]
    w = ys.shape[1]
    n_cores, n_workers = _sc_workers()
    per_worker = n // n_workers
    steps = per_worker // SC_CHUNK

    @functools.partial(
        pl.kernel, mesh=_sc_mesh(),
        out_type=jax.ShapeDtypeStruct((n, w), ys.dtype),
        scratch_types=[pltpu.VMEM((SC_CHUNK,), I32), pltpu.VMEM((SC_CHUNK, w), ys.dtype),
                       pltpu.SemaphoreType.DMA],
        name="gather_rows")
    def k(ys_hbm, idx_hbm, out_hbm, idx_v, rows_v, sem):
        wid = lax.axis_index("s") * n_cores + lax.axis_index("c")

        @pl.loop(0, steps)
        def _(j):
            base = wid * per_worker + j * SC_CHUNK
            pltpu.sync_copy(idx_hbm.at[pl.ds(base, SC_CHUNK)], idx_v)
            pltpu.async_copy(ys_hbm.at[idx_v], rows_v, sem).wait()
            pltpu.sync_copy(rows_v, out_hbm.at[pl.ds(base, SC_CHUNK)])

    return k(ys, flat_idx)


def _experts_kernel(be_ref, nu_ref, x_ref, wg_ref, wu_ref, wd_ref, o_ref, wg_scr, wu_scr, wd_scr):
    i = pl.program_id(0)
    prev = be_ref[jnp.maximum(i - 1, 0)]
    fresh = jnp.logical_or(i == 0, be_ref[i] != prev)

    @pl.when(fresh)
    def _():
        wg_scr[...] = wg_ref[...].astype(BF16)
        wu_scr[...] = wu_ref[...].astype(BF16)
        wd_scr[...] = wd_ref[...].astype(BF16)

    @pl.when(i < nu_ref[0])
    def _():
        half = wg_scr.shape[0] // 2
        lo, hi = _unpack_halves(x_ref[...])
        lo = lo.astype(BF16)
        hi = hi.astype(BF16)
        g = _dot(lo, wg_scr[0:half, :]) + _dot(hi, wg_scr[half:, :])
        u = _dot(lo, wu_scr[0:half, :]) + _dot(hi, wu_scr[half:, :])
        act = (g * jax.nn.sigmoid(g) * u).astype(BF16)
        y = _dot(act, wd_scr[...])
        o_ref[...] = _pack_halves(y[:, :half], y[:, half:])

    @pl.when(i >= nu_ref[0])
    def _():
        o_ref[...] = jnp.zeros_like(o_ref)


def _experts(block_e, n_used, xs, w_gate, w_up, w_down):
    n_rows, w = xs.shape
    _, d, f = w_gate.shape
    return pl.pallas_call(
        _experts_kernel,
        grid_spec=pltpu.PrefetchScalarGridSpec(
            num_scalar_prefetch=2,
            grid=(n_rows // ROW_BLOCK,),
            in_specs=[pl.BlockSpec((ROW_BLOCK, w), lambda i, be, nu: (i, 0)),
                      pl.BlockSpec((None, d, f), lambda i, be, nu: (be[i], 0, 0)),
                      pl.BlockSpec((None, d, f), lambda i, be, nu: (be[i], 0, 0)),
                      pl.BlockSpec((None, f, d), lambda i, be, nu: (be[i], 0, 0))],
            out_specs=pl.BlockSpec((ROW_BLOCK, w), lambda i, be, nu: (i, 0)),
            scratch_shapes=[pltpu.VMEM((d, f), BF16), pltpu.VMEM((d, f), BF16), pltpu.VMEM((f, d), BF16)]),
        out_shape=jax.ShapeDtypeStruct((n_rows, w), I32),
        compiler_params=_cparams("arbitrary"),
        name="experts",
    )(block_e, n_used, xs, w_gate, w_up, w_down)


def _final_kernel(base_ref, g_ref, wt_ref, gf_ref, ng_ref, *rest):
    o_ref = rest[-1]
    half = base_ref.shape[1] // 2
    wt = wt_ref[...]
    acc_lo = jnp.zeros((base_ref.shape[0], half), F32)
    acc_hi = jnp.zeros((base_ref.shape[0], half), F32)
    for k in range(TOP_K):
        lo, hi = _unpack_halves(g_ref[k])
        wk = wt[:, k:k + 1]
        acc_lo = acc_lo + wk * lo
        acc_hi = acc_hi + wk * hi
    gf = gf_ref[0]
    x_lo = base_ref[:, :half] + gf[:, :half] * acc_lo
    x_hi = base_ref[:, half:] + gf[:, half:] * acc_hi
    ms = (jnp.sum(x_lo * x_lo, axis=-1, keepdims=True)
          + jnp.sum(x_hi * x_hi, axis=-1, keepdims=True)) * (1.0 / (2 * half))
    inv = lax.rsqrt(ms + EPS)
    o_ref[:, :half] = x_lo * inv * ng_ref[:, :half]
    o_ref[:, half:] = x_hi * inv * ng_ref[:, half:]


def _final(base, gathered, wt, gf, norm_g, seq, b0, t_total, prev):
    t, d = base.shape
    tm = 256
    per_seq = seq // tm
    in_specs = [pl.BlockSpec((tm, d), lambda i: (i, 0)),
                pl.BlockSpec((TOP_K, tm, d // 2), lambda i: (0, i, 0)),
                pl.BlockSpec((tm, LANES), lambda i: (i, 0)),
                pl.BlockSpec((1, 1, d), lambda i: (b0 + i // per_seq, 0, 0)),
                pl.BlockSpec((1, d), lambda i: (0, 0))]
    args = [base, gathered, wt, gf, norm_g.reshape(1, d)]
    aliases = {}
    if prev is not None:
        in_specs.append(pl.BlockSpec(memory_space=pl.ANY))
        args.append(prev)
        aliases = {len(args) - 1: 0}
    return pl.pallas_call(
        _final_kernel,
        grid=(t // tm,),
        in_specs=in_specs,
        out_specs=pl.BlockSpec((tm, d), lambda i: (b0 * per_seq + i, 0)),
        out_shape=jax.ShapeDtypeStruct((t_total, d), F32),
        input_output_aliases=aliases,
        compiler_params=_cparams("arbitrary"),
        name="final",
    )(*args)


def _block_table(counts, n_blocks):
    padded = (counts + ROW_BLOCK - 1) // ROW_BLOCK * ROW_BLOCK
    pend = jnp.cumsum(padded)
    pstart = (pend - padded).astype(I32)
    n_used = (pend[-1] // ROW_BLOCK).astype(I32)
    blk = jnp.arange(n_blocks, dtype=I32)
    ended = (pend[None, :] <= (blk * ROW_BLOCK)[:, None]).astype(I32)
    block_e = jnp.minimum(jnp.sum(ended, axis=1), N_EXPERTS - 1).astype(I32)
    last_e = block_e[jnp.maximum(n_used - 1, 0)]
    block_e = jnp.where(blk < n_used, block_e, last_e)
    return pstart, block_e, n_used.reshape(1)


N_SPLIT = 2


def _moe_group(out_prev, b0, nb, xf, mods, seq, p):
    sh_m, sc_m, g_m, sh_f, sc_f, g_f = mods
    d = xf.shape[1]
    t = nb * seq
    proj = _inproj(xf, sh_m, sc_m, p["norm_mix_g"].reshape(1, d), p["w_in"], seq, b0, nb)
    x1 = _mixer(proj, xf, g_m, p["conv_a_w"], p["conv_b_w"], p["conv_b_b"], p["ln_b_g"], p["ln_b_b"], p["head_g"],
                p["w_out"], seq, b0, nb)
    base, hp, idx, rank, wt, counts = _ffn_pre(x1, sh_f, sc_f, g_f, p["norm_ffn_g"].reshape(1, d), p["w_router"],
                                               p["router_bias"], p["w_shared_gate"], p["w_shared_up"],
                                               p["w_shared_down"], seq, b0)
    n_assign = t * TOP_K
    n_blocks = (n_assign + N_EXPERTS * (ROW_BLOCK - 1) + ROW_BLOCK - 1) // ROW_BLOCK
    pstart, block_e, n_used = _block_table(counts[:, 0], n_blocks)
    dest = _dest(pstart, idx, rank)
    dest3 = dest.reshape(TOP_K, t // SC_CHUNK, SC_CHUNK).transpose(1, 0, 2)
    xs = _dispatch_rows(hp, dest3, n_blocks * ROW_BLOCK)
    ys = _experts(block_e, n_used, xs, p["w_gate"], p["w_up"], p["w_down"])
    gathered = _gather_rows(ys, dest.reshape(-1)).reshape(TOP_K, t, d // 2)
    return _final(base, gathered, wt, g_f, p["norm_final_g"], seq, b0, xf.shape[0], out_prev)


def _layer(x, c, p):
    bsz, seq, d = x.shape
    xf = x.reshape(bsz * seq, d)
    mod = _adaln(c, p["w_ada"], p["b_ada"])
    mods = [m.reshape(bsz, 1, d) for m in jnp.split(mod, 6, axis=-1)]
    n_split = N_SPLIT if bsz % N_SPLIT == 0 else 1
    nb = bsz // n_split
    out = None
    for h in range(n_split):
        out = _moe_group(out, h * nb, nb, xf, mods, seq, p)
    return out.reshape(bsz, seq, d)


def kernel(x, c, w_ada, b_ada, norm_mix_g, w_in, conv_a_w, conv_b_w, conv_b_b, ln_b_g, ln_b_b, head_norm_a_g,
           head_norm_b_g, w_out, norm_ffn_g, w_router, router_bias, w_gate, w_up, w_down, w_shared_gate,
           w_shared_up, w_shared_down, norm_final_g):
    assert w_ada.shape[0] == 1, "the fused final norm assumes a single layer"
    p = dict(w_ada=w_ada[0], b_ada=b_ada[0], norm_mix_g=norm_mix_g[0], w_in=w_in[0].astype(BF16),
             conv_a_w=conv_a_w[0], conv_b_w=conv_b_w[0], conv_b_b=conv_b_b[0], ln_b_g=ln_b_g[0], ln_b_b=ln_b_b[0],
             head_g=jnp.concatenate([head_norm_a_g[0], head_norm_b_g[0]]), w_out=w_out[0].astype(BF16),
             norm_ffn_g=norm_ffn_g[0], w_router=w_router[0], router_bias=router_bias[0], w_gate=w_gate[0],
             w_up=w_up[0], w_down=w_down[0], w_shared_gate=w_shared_gate[0], w_shared_up=w_shared_up[0],
             w_shared_down=w_shared_down[0], norm_final_g=norm_final_g)
    return _layer(x, c, p)
```

```python
import functools

import jax
import jax.numpy as jnp
from jax import lax
from jax.experimental import pallas as pl
from jax.experimental.pallas import tpu as pltpu
from jax.experimental.pallas import tpu_sc as plsc

F32 = jnp.float32
BF16 = jnp.bfloat16
I32 = jnp.int32

HEAD_CH = 64
K_SHORT = 3
K_CONF = 31
N_EXPERTS = 64
TOP_K = 8
N_GROUPS = 8
TOPK_GROUPS = 4
GROUP_SIZE = N_EXPERTS // N_GROUPS
ROUTED_SCALE = 2.5
EPS = 1e-6

LANES = 128
HALO = 32
ROW_BLOCK = 512
VMEM_LIMIT = 56 * 1024 * 1024
HI_MASK = -65536
SC_CHUNK = 32


def _cparams(*sem):
    return pltpu.CompilerParams(dimension_semantics=sem, vmem_limit_bytes=VMEM_LIMIT)


def _dot(a, b):
    return jnp.dot(a, b, preferred_element_type=F32)


def _split_bf16(x):
    hi = x.astype(BF16)
    lo = (x - hi.astype(F32)).astype(BF16)
    return hi, lo


def _pack_halves(lo, hi):
    lo_b = lax.bitcast_convert_type(lo.astype(BF16).astype(F32), I32)
    hi_b = lax.bitcast_convert_type(hi.astype(BF16).astype(F32), I32)
    return lax.shift_right_logical(lo_b, 16) | (hi_b & HI_MASK)


def _unpack_halves(p):
    lo = lax.bitcast_convert_type(lax.shift_left(p, 16), F32)
    hi = lax.bitcast_convert_type(p & HI_MASK, F32)
    return lo, hi


def _adaln_kernel(c_ref, w_ref, b_ref, o_ref):
    c = c_ref[...]
    ca = c * jax.nn.sigmoid(c)
    chi, clo = _split_bf16(ca)
    whi, wlo = _split_bf16(w_ref[...])
    o_ref[...] = _dot(chi, whi) + _dot(clo, whi) + _dot(chi, wlo) + b_ref[...]


def _adaln(c, w, b):
    bsz, d = c.shape
    n = w.shape[1]
    tn = 1024
    return pl.pallas_call(
        _adaln_kernel,
        grid=(n // tn,),
        in_specs=[pl.BlockSpec((bsz, d), lambda j: (0, 0)),
                  pl.BlockSpec((d, tn), lambda j: (0, j)),
                  pl.BlockSpec((1, tn), lambda j: (0, j))],
        out_specs=pl.BlockSpec((bsz, tn), lambda j: (0, j)),
        out_shape=jax.ShapeDtypeStruct((bsz, n), F32),
        compiler_params=_cparams("arbitrary"),
        name="adaln",
    )(c, w, b.reshape(1, n))


MIX_TILE = 512
MIX_CHUNK = 256
CONV_COLS = 256
SUBLANES = 8


def _mixer_kernel(x_ref, sh_ref, sc_ref, gm_ref, g_ref, win_ref, wa_ref, wb_ref, bb_ref, lng_ref, lnb_ref,
                  hg_ref, grp_ref, grpt_ref, wout_ref, o_ref, ua_scr, ub_scr, rot_scr, zb_scr):
    da = wa_ref.shape[1]
    rc = MIX_CHUNK
    n_rot = rc + HALO - SUBLANES

    @pl.when(pl.program_id(1) == 0)
    def _():
        ua_scr[0:HALO, :] = jnp.zeros((HALO, da), F32)
        ub_scr[0:HALO, :] = jnp.zeros((HALO, da), F32)

    for c in range(MIX_TILE // rc):
        rows = slice(c * rc, (c + 1) * rc)
        x = x_ref[rows, :]
        ms = jnp.mean(x * x, axis=-1, keepdims=True)
        h = (x * lax.rsqrt(ms + EPS) * g_ref[...] * (1.0 + sc_ref[0]) + sh_ref[0]).astype(BF16)

        def proj(k):
            return _dot(h, win_ref[:, k * da:(k + 1) * da])

        ua_scr[HALO:HALO + rc, :] = proj(2) * proj(0)
        ub_scr[HALO:HALO + rc, :] = proj(3) * jax.nn.sigmoid(proj(4))

        conv_a = jnp.zeros((rc, da), F32)
        for k in range(K_SHORT):
            off = HALO - (K_SHORT - 1) + k
            conv_a = conv_a + wa_ref[k:k + 1, :] * ua_scr[off:off + rc, :]
        ya = proj(1) * conv_a

        for cc in range(da // CONV_COLS):
            cols = slice(cc * CONV_COLS, (cc + 1) * CONV_COLS)
            for r in range(1, SUBLANES):
                rot_scr[r - 1, :, :] = ub_scr[r:r + n_rot, cols]
            acc = jnp.zeros((rc, CONV_COLS), F32) + bb_ref[:, cols]
            for k in range(K_CONF):
                q, r = divmod(HALO - (K_CONF - 1) + k, SUBLANES)
                if r == 0:
                    src = ub_scr[q * SUBLANES:q * SUBLANES + rc, cols]
                else:
                    src = rot_scr[r - 1, q * SUBLANES:q * SUBLANES + rc, :]
                acc = acc + wb_ref[k:k + 1, cols] * src
            zb_scr[:, cols] = acc

        ua_scr[0:HALO, :] = ua_scr[rc:rc + HALO, :]
        ub_scr[0:HALO, :] = ub_scr[rc:rc + HALO, :]

        zb = zb_scr[...]
        mu = jnp.mean(zb, axis=-1, keepdims=True)
        zc = zb - mu
        var = jnp.mean(zc * zc, axis=-1, keepdims=True)
        zn = zc * lax.rsqrt(var + EPS) * lng_ref[...] + lnb_ref[...]
        zs = zn * jax.nn.sigmoid(zn)

        y = jnp.concatenate([ya, zs], axis=-1)
        gsum = _dot((y * y).astype(BF16), grp_ref[...])
        scale = lax.rsqrt(gsum * (1.0 / HEAD_CH) + EPS)
        scale_full = _dot(jnp.concatenate(_split_bf16(scale), axis=-1), grpt_ref[...])
        yn = (y * scale_full * hg_ref[...]).astype(BF16)
        o_ref[rows, :] = x + gm_ref[0] * _dot(yn, wout_ref[...])


def _mixer(xf, sh, sc, g_m, norm_g, w_in_bf, conv_a_w, conv_b_w, conv_b_b, ln_g, ln_b, head_g, w_out_bf,
           seq, b0, nb):
    d = xf.shape[1]
    t = nb * seq
    da = conv_a_w.shape[1]
    ts = MIX_TILE
    per_seq = seq // ts
    n_heads = d // HEAD_CH
    head_of = jnp.arange(d, dtype=I32) // HEAD_CH
    grp = (head_of[:, None] == jnp.arange(n_heads, dtype=I32)[None, :]).astype(BF16)
    const = lambda shape: pl.BlockSpec(shape, lambda b, s: (0,) * len(shape))
    resident = lambda shape: pl.BlockSpec(shape, lambda b, s: (0,) * len(shape), pipeline_mode=pl.Buffered(1))
    vec = pl.BlockSpec((1, 1, d), lambda b, s: (b0 + b, 0, 0))
    n_rot = MIX_CHUNK + HALO - SUBLANES
    return pl.pallas_call(
        _mixer_kernel,
        grid=(nb, per_seq),
        in_specs=[
            pl.BlockSpec((ts, d), lambda b, s: ((b0 + b) * per_seq + s, 0)), vec, vec, vec, const((1, d)),
            resident(w_in_bf.shape),
            const((K_SHORT, da)), const((K_CONF, da)), const((1, da)), const((1, da)), const((1, da)),
            const((1, d)), const((d, n_heads)), const((2 * n_heads, d)), resident((d, d)),
        ],
        out_specs=pl.BlockSpec((ts, d), lambda b, s: (b * per_seq + s, 0)),
        out_shape=jax.ShapeDtypeStruct((t, d), F32),
        scratch_shapes=[pltpu.VMEM((HALO + MIX_CHUNK, da), F32), pltpu.VMEM((HALO + MIX_CHUNK, da), F32),
                        pltpu.VMEM((SUBLANES - 1, n_rot, CONV_COLS), F32), pltpu.VMEM((MIX_CHUNK, da), F32)],
        compiler_params=_cparams("arbitrary", "arbitrary"),
        name="mixer",
    )(xf, sh, sc, g_m, norm_g, w_in_bf, conv_a_w, conv_b_w, conv_b_b.reshape(1, da), ln_g.reshape(1, da),
      ln_b.reshape(1, da), head_g.reshape(1, d), grp, jnp.concatenate([grp.T, grp.T], axis=0), w_out_bf)


def _first_argmax(vals, iota, size):
    m = jnp.max(vals, axis=0, keepdims=True)
    idx = jnp.min(jnp.where(vals == m, iota, size), axis=0, keepdims=True)
    return m, idx


def _ffn_pre_kernel(x_ref, sh_ref, sc_ref, gf_ref, g_ref, wrh_ref, wrl_ref, rb_ref, tri_ref,
                    wsg_ref, wsu_ref, wsd_ref,
                    base_ref, hp_ref, idx_ref, rank_ref, wt_ref, cnt_ref, carry_scr):
    tm, d = x_ref.shape
    half = d // 2

    @pl.when(pl.program_id(0) == 0)
    def _():
        carry_scr[...] = jnp.zeros_like(carry_scr)

    x = x_ref[...]
    ms = jnp.mean(x * x, axis=-1, keepdims=True)
    h = x * lax.rsqrt(ms + EPS) * g_ref[...] * (1.0 + sc_ref[0]) + sh_ref[0]
    h_hi, h_lo = _split_bf16(h)
    hp_ref[...] = _pack_halves(h[:, :half], h[:, half:])

    nt = (((1,), (1,)), ((), ()))
    logits = (lax.dot_general(wrh_ref[...], h_hi, nt, preferred_element_type=F32)
              + lax.dot_general(wrh_ref[...], h_lo, nt, preferred_element_type=F32)
              + lax.dot_general(wrl_ref[...], h_hi, nt, preferred_element_type=F32))
    scores = jax.nn.sigmoid(logits)
    biased = scores + rb_ref[...]

    b3 = biased.reshape(N_GROUPS, GROUP_SIZE, tm)
    sub_iota = lax.broadcasted_iota(I32, (N_GROUPS, GROUP_SIZE, tm), 1)
    m1 = jnp.max(b3, axis=1, keepdims=True)
    i1 = jnp.min(jnp.where(b3 == m1, sub_iota, GROUP_SIZE), axis=1, keepdims=True)
    m2 = jnp.max(jnp.where(sub_iota == i1, -jnp.inf, b3), axis=1, keepdims=True)
    gscore = (m1 + m2).reshape(N_GROUPS, tm)

    g_iota = lax.broadcasted_iota(I32, (N_GROUPS, tm), 0)
    gsel = jnp.zeros((N_GROUPS, tm), jnp.bool_)
    gwork = gscore
    for _ in range(TOPK_GROUPS):
        _, gi = _first_argmax(gwork, g_iota, N_GROUPS)
        hit = g_iota == gi
        gsel = gsel | hit
        gwork = jnp.where(hit, -jnp.inf, gwork)
    emask = jnp.broadcast_to(gsel.reshape(N_GROUPS, 1, tm), (N_GROUPS, GROUP_SIZE, tm)).reshape(N_EXPERTS, tm)

    e_iota = lax.broadcasted_iota(I32, (N_EXPERTS, tm), 0)
    work = jnp.where(emask, biased, -jnp.inf)
    chosen = jnp.zeros((N_EXPERTS, tm), jnp.bool_)
    idx_rows, w_rows = [], []
    for _ in range(TOP_K):
        _, ei = _first_argmax(work, e_iota, N_EXPERTS)
        hit = e_iota == ei
        chosen = chosen | hit
        work = jnp.where(hit, -jnp.inf, work)
        idx_rows.append(ei)
        w_rows.append(jnp.sum(jnp.where(hit, scores, 0.0), axis=0, keepdims=True))
    idx = jnp.concatenate(idx_rows, axis=0)
    w = jnp.concatenate(w_rows, axis=0)
    w = w / jnp.sum(w, axis=0, keepdims=True) * ROUTED_SCALE

    onehot = chosen.astype(BF16)
    before = _dot(onehot, tri_ref[...]) + carry_scr[:, 0:1]
    rank_rows = [jnp.sum(jnp.where(e_iota == idx_rows[k], before, 0.0), axis=0, keepdims=True)
                 for k in range(TOP_K)]
    carry_new = carry_scr[...] + jnp.sum(chosen.astype(F32), axis=1, keepdims=True)
    carry_scr[...] = carry_new
    cnt_ref[...] = carry_new.astype(I32)
    idx_ref[...] = idx
    rank_ref[...] = jnp.concatenate(rank_rows, axis=0).astype(I32)
    w_pad = jnp.concatenate([w, jnp.zeros((LANES - TOP_K, tm), F32)], axis=0)
    wt_ref[...] = w_pad.T

    hb = h_hi
    act = jax.nn.silu(_dot(hb, wsg_ref[...])) * _dot(hb, wsu_ref[...])
    base_ref[...] = x + gf_ref[0] * _dot(act.astype(BF16), wsd_ref[...])


def _ffn_pre(x1, sh, sc, gf, g, w_router, router_bias, wsg, wsu, wsd, seq, b0):
    t, d = x1.shape
    f = wsg.shape[1]
    tm = 512
    per_seq = seq // tm
    wr_hi, wr_lo = _split_bf16(w_router.T)
    tri = (jnp.arange(tm, dtype=I32)[:, None] < jnp.arange(tm, dtype=I32)[None, :]).astype(BF16)
    vec = pl.BlockSpec((1, 1, d), lambda i: (b0 + i // per_seq, 0, 0))
    const = lambda shape: pl.BlockSpec(shape, lambda i: (0,) * len(shape))
    return pl.pallas_call(
        _ffn_pre_kernel,
        grid=(t // tm,),
        in_specs=[pl.BlockSpec((tm, d), lambda i: (i, 0)), vec, vec, vec, const((1, d)),
                  const((N_EXPERTS, d)), const((N_EXPERTS, d)), const((N_EXPERTS, 1)), const((tm, tm)),
                  const((d, f)), const((d, f)), const((f, d))],
        out_specs=[pl.BlockSpec((tm, d), lambda i: (i, 0)),
                   pl.BlockSpec((tm, d // 2), lambda i: (i, 0)),
                   pl.BlockSpec((TOP_K, tm), lambda i: (0, i)),
                   pl.BlockSpec((TOP_K, tm), lambda i: (0, i)),
                   pl.BlockSpec((tm, LANES), lambda i: (i, 0)),
                   const((N_EXPERTS, LANES))],
        out_shape=[jax.ShapeDtypeStruct((t, d), F32),
                   jax.ShapeDtypeStruct((t, d // 2), I32),
                   jax.ShapeDtypeStruct((TOP_K, t), I32),
                   jax.ShapeDtypeStruct((TOP_K, t), I32),
                   jax.ShapeDtypeStruct((t, LANES), F32),
                   jax.ShapeDtypeStruct((N_EXPERTS, LANES), I32)],
        scratch_shapes=[pltpu.VMEM((N_EXPERTS, LANES), F32)],
        compiler_params=_cparams("arbitrary"),
        name="ffn_pre",
    )(x1, sh, sc, gf, g, wr_hi, wr_lo, router_bias.reshape(N_EXPERTS, 1), tri,
      wsg.astype(BF16), wsu.astype(BF16), wsd.astype(BF16))


def _dest_kernel(pstart_ref, idx_ref, rank_ref, o_ref):
    idx = idx_ref[...]
    acc = rank_ref[...]
    for e in range(N_EXPERTS):
        acc = acc + jnp.where(idx == e, pstart_ref[e], 0)
    o_ref[...] = acc


def _dest(pstart, idx, rank):
    k, t = idx.shape
    tl = min(4096, t)
    return pl.pallas_call(
        _dest_kernel,
        grid_spec=pltpu.PrefetchScalarGridSpec(
            num_scalar_prefetch=1,
            grid=(t // tl,),
            in_specs=[pl.BlockSpec((k, tl), lambda i, ps: (0, i)),
                      pl.BlockSpec((k, tl), lambda i, ps: (0, i))],
            out_specs=pl.BlockSpec((k, tl), lambda i, ps: (0, i))),
        out_shape=jax.ShapeDtypeStruct((k, t), I32),
        compiler_params=_cparams("arbitrary"),
        name="dest",
    )(pstart, idx, rank)


def _sc_mesh():
    return plsc.VectorSubcoreMesh(core_axis_name="c", subcore_axis_name="s")


def _sc_workers():
    info = plsc.get_sparse_core_info()
    return info.num_cores, info.num_cores * info.num_subcores


def _dispatch_rows(hp, dest3, n_rows):
    t, w = hp.shape
    n_chunks, top_k, chunk = dest3.shape
    n_cores, n_workers = _sc_workers()
    per_worker = n_chunks // n_workers

    @functools.partial(
        pl.kernel, mesh=_sc_mesh(),
        out_type=jax.ShapeDtypeStruct((n_rows, w), hp.dtype),
        scratch_types=[pltpu.VMEM((top_k, chunk), I32), pltpu.VMEM((chunk, w), hp.dtype),
                       pltpu.SemaphoreType.DMA],
        name="dispatch_rows")
    def k(hp_hbm, dest_hbm, xs_hbm, idx_v, rows_v, sem):
        wid = lax.axis_index("s") * n_cores + lax.axis_index("c")

        @pl.loop(0, per_worker)
        def _(j):
            c = wid * per_worker + j
            pltpu.sync_copy(dest_hbm.at[c], idx_v)
            pltpu.sync_copy(hp_hbm.at[pl.ds(c * chunk, chunk)], rows_v)
            copies = [pltpu.async_copy(rows_v, xs_hbm.at[idx_v.at[q]], sem) for q in range(top_k)]
            for cp in copies:
                cp.wait()

    return k(hp, dest3)


def _gather_rows(ys, flat_idx):
    n = flat_idx.shape[0]
    w = ys.shape[1]
    n_cores, n_workers = _sc_workers()
    per_worker = n // n_workers
    steps = per_worker // SC_CHUNK

    @functools.partial(
        pl.kernel, mesh=_sc_mesh(),
        out_type=jax.ShapeDtypeStruct((n, w), ys.dtype),
        scratch_types=[pltpu.VMEM((SC_CHUNK,), I32), pltpu.VMEM((SC_CHUNK, w), ys.dtype),
                       pltpu.SemaphoreType.DMA],
        name="gather_rows")
    def k(ys_hbm, idx_hbm, out_hbm, idx_v, rows_v, sem):
        wid = lax.axis_index("s") * n_cores + lax.axis_index("c")

        @pl.loop(0, steps)
        def _(j):
            base = wid * per_worker + j * SC_CHUNK
            pltpu.sync_copy(idx_hbm.at[pl.ds(base, SC_CHUNK)], idx_v)
            pltpu.async_copy(ys_hbm.at[idx_v], rows_v, sem).wait()
            pltpu.sync_copy(rows_v, out_hbm.at[pl.ds(base, SC_CHUNK)])

    return k(ys, flat_idx)


def _experts_kernel(be_ref, nxt_ref, nu_ref, x_ref, wg_hbm, wu_hbm, wd_hbm, o_ref,
                    wg_stage, wu_stage, wd_stage, wg_scr, wu_scr, wd_scr, sem):
    i = pl.program_id(0)
    e = be_ref[i]
    prev = be_ref[jnp.maximum(i - 1, 0)]
    fresh = jnp.logical_or(i == 0, e != prev)

    def weight_copies(expert):
        return (pltpu.make_async_copy(wg_hbm.at[expert], wg_stage, sem.at[0]),
                pltpu.make_async_copy(wu_hbm.at[expert], wu_stage, sem.at[1]),
                pltpu.make_async_copy(wd_hbm.at[expert], wd_stage, sem.at[2]))

    @pl.when(i == 0)
    def _():
        for cp in weight_copies(e):
            cp.start()

    @pl.when(fresh)
    def _():
        for cp in weight_copies(e):
            cp.wait()
        wg_scr[...] = wg_stage[...].astype(BF16)
        wu_scr[...] = wu_stage[...].astype(BF16)
        wd_scr[...] = wd_stage[...].astype(BF16)

    @pl.when(jnp.logical_and(fresh, nxt_ref[i] >= 0))
    def _():
        for cp in weight_copies(nxt_ref[i]):
            cp.start()

    @pl.when(i < nu_ref[0])
    def _():
        half = wg_scr.shape[0] // 2
        lo, hi = _unpack_halves(x_ref[...])
        lo = lo.astype(BF16)
        hi = hi.astype(BF16)
        g = _dot(lo, wg_scr[0:half, :]) + _dot(hi, wg_scr[half:, :])
        u = _dot(lo, wu_scr[0:half, :]) + _dot(hi, wu_scr[half:, :])
        act = (g * jax.nn.sigmoid(g) * u).astype(BF16)
        y = _dot(act, wd_scr[...])
        o_ref[...] = _pack_halves(y[:, :half], y[:, half:])

    @pl.when(i >= nu_ref[0])
    def _():
        o_ref[...] = jnp.zeros_like(o_ref)


def _experts(block_e, next_e, n_used, xs, w_gate, w_up, w_down):
    n_rows, w = xs.shape
    _, d, f = w_gate.shape
    anywhere = pl.BlockSpec(memory_space=pl.ANY)
    return pl.pallas_call(
        _experts_kernel,
        grid_spec=pltpu.PrefetchScalarGridSpec(
            num_scalar_prefetch=3,
            grid=(n_rows // ROW_BLOCK,),
            in_specs=[pl.BlockSpec((ROW_BLOCK, w), lambda i, be, nx, nu: (i, 0)), anywhere, anywhere, anywhere],
            out_specs=pl.BlockSpec((ROW_BLOCK, w), lambda i, be, nx, nu: (i, 0)),
            scratch_shapes=[pltpu.VMEM((d, f), F32), pltpu.VMEM((d, f), F32), pltpu.VMEM((f, d), F32),
                            pltpu.VMEM((d, f), BF16), pltpu.VMEM((d, f), BF16), pltpu.VMEM((f, d), BF16),
                            pltpu.SemaphoreType.DMA((3,))]),
        out_shape=jax.ShapeDtypeStruct((n_rows, w), I32),
        compiler_params=_cparams("arbitrary"),
        name="experts",
    )(block_e, next_e, n_used, xs, w_gate, w_up, w_down)


def _final_kernel(base_ref, g_ref, wt_ref, gf_ref, ng_ref, *rest):
    o_ref = rest[-1]
    half = base_ref.shape[1] // 2
    wt = wt_ref[...]
    acc_lo = jnp.zeros((base_ref.shape[0], half), F32)
    acc_hi = jnp.zeros((base_ref.shape[0], half), F32)
    for k in range(TOP_K):
        lo, hi = _unpack_halves(g_ref[k])
        wk = wt[:, k:k + 1]
        acc_lo = acc_lo + wk * lo
        acc_hi = acc_hi + wk * hi
    gf = gf_ref[0]
    x_lo = base_ref[:, :half] + gf[:, :half] * acc_lo
    x_hi = base_ref[:, half:] + gf[:, half:] * acc_hi
    ms = (jnp.sum(x_lo * x_lo, axis=-1, keepdims=True)
          + jnp.sum(x_hi * x_hi, axis=-1, keepdims=True)) * (1.0 / (2 * half))
    inv = lax.rsqrt(ms + EPS)
    o_ref[:, :half] = x_lo * inv * ng_ref[:, :half]
    o_ref[:, half:] = x_hi * inv * ng_ref[:, half:]


def _final(base, gathered, wt, gf, norm_g, seq, b0, t_total, prev):
    t, d = base.shape
    tm = 256
    per_seq = seq // tm
    in_specs = [pl.BlockSpec((tm, d), lambda i: (i, 0)),
                pl.BlockSpec((TOP_K, tm, d // 2), lambda i: (0, i, 0)),
                pl.BlockSpec((tm, LANES), lambda i: (i, 0)),
                pl.BlockSpec((1, 1, d), lambda i: (b0 + i // per_seq, 0, 0)),
                pl.BlockSpec((1, d), lambda i: (0, 0))]
    args = [base, gathered, wt, gf, norm_g.reshape(1, d)]
    aliases = {}
    if prev is not None:
        in_specs.append(pl.BlockSpec(memory_space=pl.ANY))
        args.append(prev)
        aliases = {len(args) - 1: 0}
    return pl.pallas_call(
        _final_kernel,
        grid=(t // tm,),
        in_specs=in_specs,
        out_specs=pl.BlockSpec((tm, d), lambda i: (b0 * per_seq + i, 0)),
        out_shape=jax.ShapeDtypeStruct((t_total, d), F32),
        input_output_aliases=aliases,
        compiler_params=_cparams("arbitrary"),
        name="final",
    )(*args)


def _block_table(counts, n_blocks):
    padded = (counts + ROW_BLOCK - 1) // ROW_BLOCK * ROW_BLOCK
    pend = jnp.cumsum(padded)
    pstart = (pend - padded).astype(I32)
    n_used = (pend[-1] // ROW_BLOCK).astype(I32)
    blk = jnp.arange(n_blocks, dtype=I32)
    ended = (pend[None, :] <= (blk * ROW_BLOCK)[:, None]).astype(I32)
    block_e = jnp.minimum(jnp.sum(ended, axis=1), N_EXPERTS - 1).astype(I32)
    last_e = block_e[jnp.maximum(n_used - 1, 0)]
    block_e = jnp.where(blk < n_used, block_e, last_e)
    ex = jnp.arange(N_EXPERTS, dtype=I32)
    later = jnp.logical_and(ex[None, :] > ex[:, None], padded[None, :] > 0)
    nxt = jnp.min(jnp.where(later, ex[None, :], N_EXPERTS), axis=1)
    nxt = jnp.where(nxt == N_EXPERTS, -1, nxt).astype(I32)
    next_e = jnp.sum(jnp.where(block_e[:, None] == ex[None, :], nxt[None, :], 0), axis=1).astype(I32)
    return pstart, block_e, next_e, n_used.reshape(1)


N_SPLIT = 2


def _moe_group(out_prev, b0, nb, xf, mods, seq, p):
    sh_m, sc_m, g_m, sh_f, sc_f, g_f = mods
    d = xf.shape[1]
    t = nb * seq
    x1 = _mixer(xf, sh_m, sc_m, g_m, p["norm_mix_g"].reshape(1, d), p["w_in"], p["conv_a_w"], p["conv_b_w"],
                p["conv_b_b"], p["ln_b_g"], p["ln_b_b"], p["head_g"], p["w_out"], seq, b0, nb)
    base, hp, idx, rank, wt, counts = _ffn_pre(x1, sh_f, sc_f, g_f, p["norm_ffn_g"].reshape(1, d), p["w_router"],
                                               p["router_bias"], p["w_shared_gate"], p["w_shared_up"],
                                               p["w_shared_down"], seq, b0)
    n_assign = t * TOP_K
    n_blocks = (n_assign + N_EXPERTS * (ROW_BLOCK - 1) + ROW_BLOCK - 1) // ROW_BLOCK
    pstart, block_e, next_e, n_used = _block_table(counts[:, 0], n_blocks)
    dest = _dest(pstart, idx, rank)
    dest3 = dest.reshape(TOP_K, t // SC_CHUNK, SC_CHUNK).transpose(1, 0, 2)
    xs = _dispatch_rows(hp, dest3, n_blocks * ROW_BLOCK)
    ys = _experts(block_e, next_e, n_used, xs, p["w_gate"], p["w_up"], p["w_down"])
    gathered = _gather_rows(ys, dest.reshape(-1)).reshape(TOP_K, t, d // 2)
    return _final(base, gathered, wt, g_f, p["norm_final_g"], seq, b0, xf.shape[0], out_prev)


def _layer(x, c, p):
    bsz, seq, d = x.shape
    xf = x.reshape(bsz * seq, d)
    mod = _adaln(c, p["w_ada"], p["b_ada"])
    mods = [m.reshape(bsz, 1, d) for m in jnp.split(mod, 6, axis=-1)]
    n_split = N_SPLIT if bsz % N_SPLIT == 0 else 1
    nb = bsz // n_split
    out = None
    for h in range(n_split):
        out = _moe_group(out, h * nb, nb, xf, mods, seq, p)
    return out.reshape(bsz, seq, d)


def kernel(x, c, w_ada, b_ada, norm_mix_g, w_in, conv_a_w, conv_b_w, conv_b_b, ln_b_g, ln_b_b, head_norm_a_g,
           head_norm_b_g, w_out, norm_ffn_g, w_router, router_bias, w_gate, w_up, w_down, w_shared_gate,
           w_shared_up, w_shared_down, norm_final_g):
    assert w_ada.shape[0] == 1, "the fused final norm assumes a single layer"
    p = dict(w_ada=w_ada[0], b_ada=b_ada[0], norm_mix_g=norm_mix_g[0], w_in=w_in[0].astype(BF16),
             conv_a_w=conv_a_w[0], conv_b_w=conv_b_w[0], conv_b_b=conv_b_b[0], ln_b_g=ln_b_g[0], ln_b_b=ln_b_b[0],
             head_g=jnp.concatenate([head_norm_a_g[0], head_norm_b_g[0]]), w_out=w_out[0].astype(BF16),
             norm_ffn_g=norm_ffn_g[0], w_router=w_router[0], router_bias=router_bias[0], w_gate=w_gate[0],
             w_up=w_up[0], w_down=w_down[0], w_shared_gate=w_shared_gate[0], w_shared_up=w_shared_up[0],
             w_shared_down=w_shared_down[0], norm_final_g=norm_final_g)
    return _layer(x, c, p)
```

```python
import functools

import jax
import jax.numpy as jnp
from jax import lax
from jax.experimental import pallas as pl
from jax.experimental.pallas import tpu as pltpu
from jax.experimental.pallas import tpu_sc as plsc

F32 = jnp.float32
BF16 = jnp.bfloat16
I32 = jnp.int32

HEAD_CH = 64
K_SHORT = 3
K_CONF = 31
N_EXPERTS = 64
TOP_K = 8
N_GROUPS = 8
TOPK_GROUPS = 4
GROUP_SIZE = N_EXPERTS // N_GROUPS
ROUTED_SCALE = 2.5
EPS = 1e-6

LANES = 128
HALO = 32
ROW_BLOCK = 512
VMEM_LIMIT = 56 * 1024 * 1024
HI_MASK = -65536
SC_CHUNK = 32


def _cparams(*sem, vmem=VMEM_LIMIT):
    return pltpu.CompilerParams(dimension_semantics=sem, vmem_limit_bytes=vmem)


def _dot(a, b):
    return jnp.dot(a, b, preferred_element_type=F32)


def _split_bf16(x):
    hi = x.astype(BF16)
    lo = (x - hi.astype(F32)).astype(BF16)
    return hi, lo


def _pack_halves(lo, hi):
    lo_b = lax.bitcast_convert_type(lo.astype(BF16).astype(F32), I32)
    hi_b = lax.bitcast_convert_type(hi.astype(BF16).astype(F32), I32)
    return lax.shift_right_logical(lo_b, 16) | (hi_b & HI_MASK)


def _unpack_halves(p):
    lo = lax.bitcast_convert_type(lax.shift_left(p, 16), F32)
    hi = lax.bitcast_convert_type(p & HI_MASK, F32)
    return lo, hi


def _adaln_kernel(c_ref, w_ref, b_ref, o_ref):
    c = c_ref[...]
    ca = c * jax.nn.sigmoid(c)
    chi, clo = _split_bf16(ca)
    whi, wlo = _split_bf16(w_ref[...])
    o_ref[...] = _dot(chi, whi) + _dot(clo, whi) + _dot(chi, wlo) + b_ref[...]


def _adaln(c, w, b):
    bsz, d = c.shape
    n = w.shape[1]
    tn = 1024
    return pl.pallas_call(
        _adaln_kernel,
        grid=(n // tn,),
        in_specs=[pl.BlockSpec((bsz, d), lambda j: (0, 0)),
                  pl.BlockSpec((d, tn), lambda j: (0, j)),
                  pl.BlockSpec((1, tn), lambda j: (0, j))],
        out_specs=pl.BlockSpec((bsz, tn), lambda j: (0, j)),
        out_shape=jax.ShapeDtypeStruct((bsz, n), F32),
        compiler_params=_cparams("arbitrary"),
        name="adaln",
    )(c, w, b.reshape(1, n))


MIX_TILE = 512
MIX_CHUNK = 256
CONV_COLS = 256
SUBLANES = 8
MIXER_VMEM_LIMIT = 62 * 1024 * 1024


def _mixer_kernel(x_ref, sh_ref, sc_ref, gm_ref, g_ref, win_ref, wa_ref, wb_ref, bb_ref, lng_ref, lnb_ref,
                  hg_ref, grp_ref, grpt_ref, wout_ref, o_ref, ua_scr, ub_scr, rot_scr, zb_scr, st_scr, h_scr):
    da = wa_ref.shape[1]
    rc = MIX_CHUNK
    n_rot = rc + HALO - SUBLANES

    @pl.when(pl.program_id(1) == 0)
    def _():
        ua_scr[0:HALO, :] = jnp.zeros((HALO, da), F32)
        ub_scr[0:HALO, :] = jnp.zeros((HALO, da), F32)

    x = x_ref[...]
    ms = jnp.mean(x * x, axis=-1, keepdims=True)
    h_scr[...] = (x * lax.rsqrt(ms + EPS) * g_ref[...] * (1.0 + sc_ref[0]) + sh_ref[0]).astype(BF16)

    def in_proj(c):
        h = h_scr[c * rc:(c + 1) * rc, :]

        def proj(k):
            return _dot(h, win_ref[:, k * da:(k + 1) * da])

        st_scr[0] = proj(2) * proj(0)
        st_scr[1] = proj(3) * jax.nn.sigmoid(proj(4))
        st_scr[2] = proj(1)

    def unstage():
        ua_scr[HALO:HALO + rc, :] = st_scr[0]
        ub_scr[HALO:HALO + rc, :] = st_scr[1]
        return st_scr[2]

    def convs_and_norms(ba):

        conv_a = jnp.zeros((rc, da), F32)
        for k in range(K_SHORT):
            off = HALO - (K_SHORT - 1) + k
            conv_a = conv_a + wa_ref[k:k + 1, :] * ua_scr[off:off + rc, :]
        ya = ba * conv_a

        for cc in range(da // CONV_COLS):
            cols = slice(cc * CONV_COLS, (cc + 1) * CONV_COLS)
            for r in range(1, SUBLANES):
                rot_scr[r - 1, :, :] = ub_scr[r:r + n_rot, cols]
            acc = jnp.zeros((rc, CONV_COLS), F32) + bb_ref[:, cols]
            for k in range(K_CONF):
                q, r = divmod(HALO - (K_CONF - 1) + k, SUBLANES)
                if r == 0:
                    src = ub_scr[q * SUBLANES:q * SUBLANES + rc, cols]
                else:
                    src = rot_scr[r - 1, q * SUBLANES:q * SUBLANES + rc, :]
                acc = acc + wb_ref[k:k + 1, cols] * src
            zb_scr[:, cols] = acc

        ua_scr[0:HALO, :] = ua_scr[rc:rc + HALO, :]
        ub_scr[0:HALO, :] = ub_scr[rc:rc + HALO, :]

        zb = zb_scr[...]
        mu = jnp.mean(zb, axis=-1, keepdims=True)
        zc = zb - mu
        var = jnp.mean(zc * zc, axis=-1, keepdims=True)
        zn = zc * lax.rsqrt(var + EPS) * lng_ref[...] + lnb_ref[...]
        zs = zn * jax.nn.sigmoid(zn)

        y = jnp.concatenate([ya, zs], axis=-1)
        gsum = _dot((y * y).astype(BF16), grp_ref[...])
        scale = lax.rsqrt(gsum * (1.0 / HEAD_CH) + EPS)
        scale_full = _dot(jnp.concatenate(_split_bf16(scale), axis=-1), grpt_ref[...])
        return (y * scale_full * hg_ref[...]).astype(BF16)

    n_chunks = MIX_TILE // rc
    in_proj(0)
    for c in range(n_chunks):
        ba = unstage()
        if c + 1 < n_chunks:
            in_proj(c + 1)
        yn = convs_and_norms(ba)
        rows = slice(c * rc, (c + 1) * rc)
        o_ref[rows, :] = x_ref[rows, :] + gm_ref[0] * _dot(yn, wout_ref[...])


def _mixer(xf, sh, sc, g_m, norm_g, w_in_bf, conv_a_w, conv_b_w, conv_b_b, ln_g, ln_b, head_g, w_out_bf,
           seq, b0, nb):
    d = xf.shape[1]
    t = nb * seq
    da = conv_a_w.shape[1]
    ts = MIX_TILE
    per_seq = seq // ts
    n_heads = d // HEAD_CH
    head_of = jnp.arange(d, dtype=I32) // HEAD_CH
    grp = (head_of[:, None] == jnp.arange(n_heads, dtype=I32)[None, :]).astype(BF16)
    const = lambda shape: pl.BlockSpec(shape, lambda b, s: (0,) * len(shape))
    resident = lambda shape: pl.BlockSpec(shape, lambda b, s: (0,) * len(shape), pipeline_mode=pl.Buffered(1))
    vec = pl.BlockSpec((1, 1, d), lambda b, s: (b0 + b, 0, 0))
    n_rot = MIX_CHUNK + HALO - SUBLANES
    return pl.pallas_call(
        _mixer_kernel,
        grid=(nb, per_seq),
        in_specs=[
            pl.BlockSpec((ts, d), lambda b, s: ((b0 + b) * per_seq + s, 0)), vec, vec, vec, const((1, d)),
            resident(w_in_bf.shape),
            const((K_SHORT, da)), const((K_CONF, da)), const((1, da)), const((1, da)), const((1, da)),
            const((1, d)), const((d, n_heads)), const((2 * n_heads, d)), resident((d, d)),
        ],
        out_specs=pl.BlockSpec((ts, d), lambda b, s: (b * per_seq + s, 0)),
        out_shape=jax.ShapeDtypeStruct((t, d), F32),
        scratch_shapes=[pltpu.VMEM((HALO + MIX_CHUNK, da), F32), pltpu.VMEM((HALO + MIX_CHUNK, da), F32),
                        pltpu.VMEM((SUBLANES - 1, n_rot, CONV_COLS), F32), pltpu.VMEM((MIX_CHUNK, da), F32),
                        pltpu.VMEM((3, MIX_CHUNK, da), F32), pltpu.VMEM((MIX_TILE, d), BF16)],
        compiler_params=_cparams("arbitrary", "arbitrary", vmem=MIXER_VMEM_LIMIT),
        name="mixer",
    )(xf, sh, sc, g_m, norm_g, w_in_bf, conv_a_w, conv_b_w, conv_b_b.reshape(1, da), ln_g.reshape(1, da),
      ln_b.reshape(1, da), head_g.reshape(1, d), grp, jnp.concatenate([grp.T, grp.T], axis=0), w_out_bf)


def _first_argmax(vals, iota, size):
    m = jnp.max(vals, axis=0, keepdims=True)
    idx = jnp.min(jnp.where(vals == m, iota, size), axis=0, keepdims=True)
    return m, idx


def _ffn_pre_kernel(x_ref, sh_ref, sc_ref, gf_ref, g_ref, wrh_ref, wrl_ref, rb_ref, tri_ref,
                    wsg_ref, wsu_ref, wsd_ref,
                    base_ref, hp_ref, idx_ref, rank_ref, wt_ref, cnt_ref, carry_scr):
    tm, d = x_ref.shape
    half = d // 2

    @pl.when(pl.program_id(0) == 0)
    def _():
        carry_scr[...] = jnp.zeros_like(carry_scr)

    x = x_ref[...]
    ms = jnp.mean(x * x, axis=-1, keepdims=True)
    h = x * lax.rsqrt(ms + EPS) * g_ref[...] * (1.0 + sc_ref[0]) + sh_ref[0]
    h_hi, h_lo = _split_bf16(h)
    hp_ref[...] = _pack_halves(h[:, :half], h[:, half:])

    nt = (((1,), (1,)), ((), ()))
    logits = (lax.dot_general(wrh_ref[...], h_hi, nt, preferred_element_type=F32)
              + lax.dot_general(wrh_ref[...], h_lo, nt, preferred_element_type=F32)
              + lax.dot_general(wrl_ref[...], h_hi, nt, preferred_element_type=F32))
    act = jax.nn.silu(_dot(h_hi, wsg_ref[...])) * _dot(h_hi, wsu_ref[...])
    base_ref[...] = x + gf_ref[0] * _dot(act.astype(BF16), wsd_ref[...])

    scores = jax.nn.sigmoid(logits)
    biased = scores + rb_ref[...]

    b3 = biased.reshape(N_GROUPS, GROUP_SIZE, tm)
    sub_iota = lax.broadcasted_iota(I32, (N_GROUPS, GROUP_SIZE, tm), 1)
    m1 = jnp.max(b3, axis=1, keepdims=True)
    i1 = jnp.min(jnp.where(b3 == m1, sub_iota, GROUP_SIZE), axis=1, keepdims=True)
    m2 = jnp.max(jnp.where(sub_iota == i1, -jnp.inf, b3), axis=1, keepdims=True)
    gscore = (m1 + m2).reshape(N_GROUPS, tm)

    g_iota = lax.broadcasted_iota(I32, (N_GROUPS, tm), 0)
    gsel = jnp.zeros((N_GROUPS, tm), jnp.bool_)
    gwork = gscore
    for _ in range(TOPK_GROUPS):
        _, gi = _first_argmax(gwork, g_iota, N_GROUPS)
        hit = g_iota == gi
        gsel = gsel | hit
        gwork = jnp.where(hit, -jnp.inf, gwork)
    emask = jnp.broadcast_to(gsel.reshape(N_GROUPS, 1, tm), (N_GROUPS, GROUP_SIZE, tm)).reshape(N_EXPERTS, tm)

    e_iota = lax.broadcasted_iota(I32, (N_EXPERTS, tm), 0)
    work = jnp.where(emask, biased, -jnp.inf)
    chosen = jnp.zeros((N_EXPERTS, tm), jnp.bool_)
    idx_rows, w_rows = [], []
    for _ in range(TOP_K):
        _, ei = _first_argmax(work, e_iota, N_EXPERTS)
        hit = e_iota == ei
        chosen = chosen | hit
        work = jnp.where(hit, -jnp.inf, work)
        idx_rows.append(ei)
        w_rows.append(jnp.sum(jnp.where(hit, scores, 0.0), axis=0, keepdims=True))
    idx = jnp.concatenate(idx_rows, axis=0)
    w = jnp.concatenate(w_rows, axis=0)
    w = w / jnp.sum(w, axis=0, keepdims=True) * ROUTED_SCALE

    onehot = chosen.astype(BF16)
    before = _dot(onehot, tri_ref[...]) + carry_scr[:, 0:1]
    rank_rows = [jnp.sum(jnp.where(e_iota == idx_rows[k], before, 0.0), axis=0, keepdims=True)
                 for k in range(TOP_K)]
    carry_new = carry_scr[...] + jnp.sum(chosen.astype(F32), axis=1, keepdims=True)
    carry_scr[...] = carry_new
    cnt_ref[...] = carry_new.astype(I32)
    idx_ref[...] = idx
    rank_ref[...] = jnp.concatenate(rank_rows, axis=0).astype(I32)
    w_pad = jnp.concatenate([w, jnp.zeros((LANES - TOP_K, tm), F32)], axis=0)
    wt_ref[...] = w_pad.T


def _ffn_pre(x1, sh, sc, gf, g, w_router, router_bias, wsg, wsu, wsd, seq, b0):
    t, d = x1.shape
    f = wsg.shape[1]
    tm = 512
    per_seq = seq // tm
    wr_hi, wr_lo = _split_bf16(w_router.T)
    tri = (jnp.arange(tm, dtype=I32)[:, None] < jnp.arange(tm, dtype=I32)[None, :]).astype(BF16)
    vec = pl.BlockSpec((1, 1, d), lambda i: (b0 + i // per_seq, 0, 0))
    const = lambda shape: pl.BlockSpec(shape, lambda i: (0,) * len(shape))
    return pl.pallas_call(
        _ffn_pre_kernel,
        grid=(t // tm,),
        in_specs=[pl.BlockSpec((tm, d), lambda i: (i, 0)), vec, vec, vec, const((1, d)),
                  const((N_EXPERTS, d)), const((N_EXPERTS, d)), const((N_EXPERTS, 1)), const((tm, tm)),
                  const((d, f)), const((d, f)), const((f, d))],
        out_specs=[pl.BlockSpec((tm, d), lambda i: (i, 0)),
                   pl.BlockSpec((tm, d // 2), lambda i: (i, 0)),
                   pl.BlockSpec((TOP_K, tm), lambda i: (0, i)),
                   pl.BlockSpec((TOP_K, tm), lambda i: (0, i)),
                   pl.BlockSpec((tm, LANES), lambda i: (i, 0)),
                   const((N_EXPERTS, LANES))],
        out_shape=[jax.ShapeDtypeStruct((t, d), F32),
                   jax.ShapeDtypeStruct((t, d // 2), I32),
                   jax.ShapeDtypeStruct((TOP_K, t), I32),
                   jax.ShapeDtypeStruct((TOP_K, t), I32),
                   jax.ShapeDtypeStruct((t, LANES), F32),
                   jax.ShapeDtypeStruct((N_EXPERTS, LANES), I32)],
        scratch_shapes=[pltpu.VMEM((N_EXPERTS, LANES), F32)],
        compiler_params=_cparams("arbitrary"),
        name="ffn_pre",
    )(x1, sh, sc, gf, g, wr_hi, wr_lo, router_bias.reshape(N_EXPERTS, 1), tri,
      wsg.astype(BF16), wsu.astype(BF16), wsd.astype(BF16))


def _dest_kernel(pstart_ref, idx_ref, rank_ref, o_ref):
    idx = idx_ref[...]
    acc = rank_ref[...]
    for e in range(N_EXPERTS):
        acc = acc + jnp.where(idx == e, pstart_ref[e], 0)
    o_ref[...] = acc


def _dest(pstart, idx, rank):
    k, t = idx.shape
    tl = min(4096, t)
    return pl.pallas_call(
        _dest_kernel,
        grid_spec=pltpu.PrefetchScalarGridSpec(
            num_scalar_prefetch=1,
            grid=(t // tl,),
            in_specs=[pl.BlockSpec((k, tl), lambda i, ps: (0, i)),
                      pl.BlockSpec((k, tl), lambda i, ps: (0, i))],
            out_specs=pl.BlockSpec((k, tl), lambda i, ps: (0, i))),
        out_shape=jax.ShapeDtypeStruct((k, t), I32),
        compiler_params=_cparams("arbitrary"),
        name="dest",
    )(pstart, idx, rank)


def _sc_mesh():
    return plsc.VectorSubcoreMesh(core_axis_name="c", subcore_axis_name="s")


def _sc_workers():
    info = plsc.get_sparse_core_info()
    return info.num_cores, info.num_cores * info.num_subcores


def _dispatch_rows(hp, dest3, n_rows):
    t, w = hp.shape
    n_chunks, top_k, chunk = dest3.shape
    n_cores, n_workers = _sc_workers()
    per_worker = n_chunks // n_workers

    @functools.partial(
        pl.kernel, mesh=_sc_mesh(),
        out_type=jax.ShapeDtypeStruct((n_rows, w), hp.dtype),
        scratch_types=[pltpu.VMEM((top_k, chunk), I32), pltpu.VMEM((chunk, w), hp.dtype),
                       pltpu.SemaphoreType.DMA],
        name="dispatch_rows")
    def k(hp_hbm, dest_hbm, xs_hbm, idx_v, rows_v, sem):
        wid = lax.axis_index("s") * n_cores + lax.axis_index("c")

        @pl.loop(0, per_worker)
        def _(j):
            c = wid * per_worker + j
            pltpu.sync_copy(dest_hbm.at[c], idx_v)
            pltpu.sync_copy(hp_hbm.at[pl.ds(c * chunk, chunk)], rows_v)
            copies = [pltpu.async_copy(rows_v, xs_hbm.at[idx_v.at[q]], sem) for q in range(top_k)]
            for cp in copies:
                cp.wait()

    return k(hp, dest3)


def _gather_rows(ys, flat_idx):
    n = flat_idx.shape[0]
    w = ys.shape[1]
    n_cores, n_workers = _sc_workers()
    per_worker = n // n_workers
    steps = per_worker // SC_CHUNK

    @functools.partial(
        pl.kernel, mesh=_sc_mesh(),
        out_type=jax.ShapeDtypeStruct((n, w), ys.dtype),
        scratch_types=[pltpu.VMEM((SC_CHUNK,), I32), pltpu.VMEM((SC_CHUNK, w), ys.dtype),
                       pltpu.SemaphoreType.DMA],
        name="gather_rows")
    def k(ys_hbm, idx_hbm, out_hbm, idx_v, rows_v, sem):
        wid = lax.axis_index("s") * n_cores + lax.axis_index("c")

        @pl.loop(0, steps)
        def _(j):
            base = wid * per_worker + j * SC_CHUNK
            pltpu.sync_copy(idx_hbm.at[pl.ds(base, SC_CHUNK)], idx_v)
            pltpu.async_copy(ys_hbm.at[idx_v], rows_v, sem).wait()
            pltpu.sync_copy(rows_v, out_hbm.at[pl.ds(base, SC_CHUNK)])

    return k(ys, flat_idx)


def _experts_kernel(be_ref, nxt_ref, nu_ref, x_ref, wg_hbm, wu_hbm, wd_hbm, o_ref,
                    wg_stage, wu_stage, wd_stage, wg_scr, wu_scr, wd_scr, sem):
    i = pl.program_id(0)
    e = be_ref[i]
    prev = be_ref[jnp.maximum(i - 1, 0)]
    fresh = jnp.logical_or(i == 0, e != prev)

    def weight_copies(expert):
        return (pltpu.make_async_copy(wg_hbm.at[expert], wg_stage, sem.at[0]),
                pltpu.make_async_copy(wu_hbm.at[expert], wu_stage, sem.at[1]),
                pltpu.make_async_copy(wd_hbm.at[expert], wd_stage, sem.at[2]))

    @pl.when(i == 0)
    def _():
        for cp in weight_copies(e):
            cp.start()

    @pl.when(fresh)
    def _():
        for cp in weight_copies(e):
            cp.wait()
        wg_scr[...] = wg_stage[...].astype(BF16)
        wu_scr[...] = wu_stage[...].astype(BF16)
        wd_scr[...] = wd_stage[...].astype(BF16)

    @pl.when(jnp.logical_and(fresh, nxt_ref[i] >= 0))
    def _():
        for cp in weight_copies(nxt_ref[i]):
            cp.start()

    @pl.when(i < nu_ref[0])
    def _():
        half = wg_scr.shape[0] // 2
        lo, hi = _unpack_halves(x_ref[...])
        lo = lo.astype(BF16)
        hi = hi.astype(BF16)
        g = _dot(lo, wg_scr[0:half, :]) + _dot(hi, wg_scr[half:, :])
        u = _dot(lo, wu_scr[0:half, :]) + _dot(hi, wu_scr[half:, :])
        act = (g * jax.nn.sigmoid(g) * u).astype(BF16)
        y = _dot(act, wd_scr[...])
        o_ref[...] = _pack_halves(y[:, :half], y[:, half:])

    @pl.when(i >= nu_ref[0])
    def _():
        o_ref[...] = jnp.zeros_like(o_ref)


def _experts(block_e, next_e, n_used, xs, w_gate, w_up, w_down):
    n_rows, w = xs.shape
    _, d, f = w_gate.shape
    anywhere = pl.BlockSpec(memory_space=pl.ANY)
    return pl.pallas_call(
        _experts_kernel,
        grid_spec=pltpu.PrefetchScalarGridSpec(
            num_scalar_prefetch=3,
            grid=(n_rows // ROW_BLOCK,),
            in_specs=[pl.BlockSpec((ROW_BLOCK, w), lambda i, be, nx, nu: (i, 0)), anywhere, anywhere, anywhere],
            out_specs=pl.BlockSpec((ROW_BLOCK, w), lambda i, be, nx, nu: (i, 0)),
            scratch_shapes=[pltpu.VMEM((d, f), F32), pltpu.VMEM((d, f), F32), pltpu.VMEM((f, d), F32),
                            pltpu.VMEM((d, f), BF16), pltpu.VMEM((d, f), BF16), pltpu.VMEM((f, d), BF16),
                            pltpu.SemaphoreType.DMA((3,))]),
        out_shape=jax.ShapeDtypeStruct((n_rows, w), I32),
        compiler_params=_cparams("arbitrary"),
        name="experts",
    )(block_e, next_e, n_used, xs, w_gate, w_up, w_down)


def _final_kernel(base_ref, g_ref, wt_ref, gf_ref, ng_ref, *rest):
    o_ref = rest[-1]
    half = base_ref.shape[1] // 2
    wt = wt_ref[...]
    acc_lo = jnp.zeros((base_ref.shape[0], half), F32)
    acc_hi = jnp.zeros((base_ref.shape[0], half), F32)
    for k in range(TOP_K):
        lo, hi = _unpack_halves(g_ref[k])
        wk = wt[:, k:k + 1]
        acc_lo = acc_lo + wk * lo
        acc_hi = acc_hi + wk * hi
    gf = gf_ref[0]
    x_lo = base_ref[:, :half] + gf[:, :half] * acc_lo
    x_hi = base_ref[:, half:] + gf[:, half:] * acc_hi
    ms = (jnp.sum(x_lo * x_lo, axis=-1, keepdims=True)
          + jnp.sum(x_hi * x_hi, axis=-1, keepdims=True)) * (1.0 / (2 * half))
    inv = lax.rsqrt(ms + EPS)
    o_ref[:, :half] = x_lo * inv * ng_ref[:, :half]
    o_ref[:, half:] = x_hi * inv * ng_ref[:, half:]


def _final(base, gathered, wt, gf, norm_g, seq, b0, t_total, prev):
    t, d = base.shape
    tm = 256
    per_seq = seq // tm
    in_specs = [pl.BlockSpec((tm, d), lambda i: (i, 0)),
                pl.BlockSpec((TOP_K, tm, d // 2), lambda i: (0, i, 0)),
                pl.BlockSpec((tm, LANES), lambda i: (i, 0)),
                pl.BlockSpec((1, 1, d), lambda i: (b0 + i // per_seq, 0, 0)),
                pl.BlockSpec((1, d), lambda i: (0, 0))]
    args = [base, gathered, wt, gf, norm_g.reshape(1, d)]
    aliases = {}
    if prev is not None:
        in_specs.append(pl.BlockSpec(memory_space=pl.ANY))
        args.append(prev)
        aliases = {len(args) - 1: 0}
    return pl.pallas_call(
        _final_kernel,
        grid=(t // tm,),
        in_specs=in_specs,
        out_specs=pl.BlockSpec((tm, d), lambda i: (b0 * per_seq + i, 0)),
        out_shape=jax.ShapeDtypeStruct((t_total, d), F32),
        input_output_aliases=aliases,
        compiler_params=_cparams("arbitrary"),
        name="final",
    )(*args)


def _block_table(counts, n_blocks):
    padded = (counts + ROW_BLOCK - 1) // ROW_BLOCK * ROW_BLOCK
    pend = jnp.cumsum(padded)
    pstart = (pend - padded).astype(I32)
    n_used = (pend[-1] // ROW_BLOCK).astype(I32)
    blk = jnp.arange(n_blocks, dtype=I32)
    ended = (pend[None, :] <= (blk * ROW_BLOCK)[:, None]).astype(I32)
    block_e = jnp.minimum(jnp.sum(ended, axis=1), N_EXPERTS - 1).astype(I32)
    last_e = block_e[jnp.maximum(n_used - 1, 0)]
    block_e = jnp.where(blk < n_used, block_e, last_e)
    ex = jnp.arange(N_EXPERTS, dtype=I32)
    later = jnp.logical_and(ex[None, :] > ex[:, None], padded[None, :] > 0)
    nxt = jnp.min(jnp.where(later, ex[None, :], N_EXPERTS), axis=1)
    nxt = jnp.where(nxt == N_EXPERTS, -1, nxt).astype(I32)
    next_e = jnp.sum(jnp.where(block_e[:, None] == ex[None, :], nxt[None, :], 0), axis=1).astype(I32)
    return pstart, block_e, next_e, n_used.reshape(1)


def _group_sizes(bsz):
    if bsz % 8 == 0:
        return (5 * bsz // 8, 3 * bsz // 8)
    if bsz % 2 == 0:
        return (bsz // 2, bsz // 2)
    return (bsz,)


def _moe_group(out_prev, b0, nb, xf, mods, seq, p):
    sh_m, sc_m, g_m, sh_f, sc_f, g_f = mods
    d = xf.shape[1]
    t = nb * seq
    x1 = _mixer(xf, sh_m, sc_m, g_m, p["norm_mix_g"].reshape(1, d), p["w_in"], p["conv_a_w"], p["conv_b_w"],
                p["conv_b_b"], p["ln_b_g"], p["ln_b_b"], p["head_g"], p["w_out"], seq, b0, nb)
    base, hp, idx, rank, wt, counts = _ffn_pre(x1, sh_f, sc_f, g_f, p["norm_ffn_g"].reshape(1, d), p["w_router"],
                                               p["router_bias"], p["w_shared_gate"], p["w_shared_up"],
                                               p["w_shared_down"], seq, b0)
    n_assign = t * TOP_K
    n_blocks = (n_assign + N_EXPERTS * (ROW_BLOCK - 1) + ROW_BLOCK - 1) // ROW_BLOCK
    pstart, block_e, next_e, n_used = _block_table(counts[:, 0], n_blocks)
    dest = _dest(pstart, idx, rank)
    dest3 = dest.reshape(TOP_K, t // SC_CHUNK, SC_CHUNK).transpose(1, 0, 2)
    xs = _dispatch_rows(hp, dest3, n_blocks * ROW_BLOCK)
    ys = _experts(block_e, next_e, n_used, xs, p["w_gate"], p["w_up"], p["w_down"])
    gathered = _gather_rows(ys, dest.reshape(-1)).reshape(TOP_K, t, d // 2)
    return _final(base, gathered, wt, g_f, p["norm_final_g"], seq, b0, xf.shape[0], out_prev)


def _layer(x, c, p):
    bsz, seq, d = x.shape
    xf = x.reshape(bsz * seq, d)
    mod = _adaln(c, p["w_ada"], p["b_ada"])
    mods = [m.reshape(bsz, 1, d) for m in jnp.split(mod, 6, axis=-1)]
    out, b0 = None, 0
    for nb in _group_sizes(bsz):
        out = _moe_group(out, b0, nb, xf, mods, seq, p)
        b0 += nb
    return out.reshape(bsz, seq, d)


def kernel(x, c, w_ada, b_ada, norm_mix_g, w_in, conv_a_w, conv_b_w, conv_b_b, ln_b_g, ln_b_b, head_norm_a_g,
           head_norm_b_g, w_out, norm_ffn_g, w_router, router_bias, w_gate, w_up, w_down, w_shared_gate,
           w_shared_up, w_shared_down, norm_final_g):
    assert w_ada.shape[0] == 1, "the fused final norm assumes a single layer"
    p = dict(w_ada=w_ada[0], b_ada=b_ada[0], norm_mix_g=norm_mix_g[0], w_in=w_in[0].astype(BF16),
             conv_a_w=conv_a_w[0], conv_b_w=conv_b_w[0], conv_b_b=conv_b_b[0], ln_b_g=ln_b_g[0], ln_b_b=ln_b_b[0],
             head_g=jnp.concatenate([head_norm_a_g[0], head_norm_b_g[0]]), w_out=w_out[0].astype(BF16),
             norm_ffn_g=norm_ffn_g[0], w_router=w_router[0], router_bias=router_bias[0], w_gate=w_gate[0],
             w_up=w_up[0], w_down=w_down[0], w_shared_gate=w_shared_gate[0], w_shared_up=w_shared_up[0],
             w_shared_down=w_shared_down[0], norm_final_g=norm_final_g)
    return _layer(x, c, p)
```

```python
import functools

import jax
import jax.numpy as jnp
from jax import lax
from jax.experimental import pallas as pl
from jax.experimental.pallas import tpu as pltpu
from jax.experimental.pallas import tpu_sc as plsc

F32 = jnp.float32
BF16 = jnp.bfloat16
I32 = jnp.int32

HEAD_CH = 64
K_SHORT = 3
K_CONF = 31
N_EXPERTS = 64
TOP_K = 8
N_GROUPS = 8
TOPK_GROUPS = 4
GROUP_SIZE = N_EXPERTS // N_GROUPS
ROUTED_SCALE = 2.5
EPS = 1e-6

LANES = 128
HALO = 32
ROW_BLOCK = 512
STEP_BLOCKS = 2
VMEM_LIMIT = 56 * 1024 * 1024
HI_MASK = -65536
SC_CHUNK = 32
ADALN_COLS = 1024
FFN_TILE = 512
DEST_TILE = 4096
FINAL_TILE = 256


def _cparams(*sem, vmem=VMEM_LIMIT):
    return pltpu.CompilerParams(dimension_semantics=sem, vmem_limit_bytes=vmem)


def _dot(a, b):
    return jnp.dot(a, b, preferred_element_type=F32)


def _split_bf16(x):
    hi = x.astype(BF16)
    lo = (x - hi.astype(F32)).astype(BF16)
    return hi, lo


def _pack_halves(lo, hi):
    lo_b = lax.bitcast_convert_type(lo.astype(BF16).astype(F32), I32)
    hi_b = lax.bitcast_convert_type(hi.astype(BF16).astype(F32), I32)
    return lax.shift_right_logical(lo_b, 16) | (hi_b & HI_MASK)


def _unpack_halves(p):
    lo = lax.bitcast_convert_type(lax.shift_left(p, 16), F32)
    hi = lax.bitcast_convert_type(p & HI_MASK, F32)
    return lo, hi


def _adaln_kernel(c_ref, w_ref, b_ref, o_ref):
    c = c_ref[...]
    ca = c * jax.nn.sigmoid(c)
    chi, clo = _split_bf16(ca)
    whi, wlo = _split_bf16(w_ref[...])
    o_ref[...] = _dot(chi, whi) + _dot(clo, whi) + _dot(chi, wlo) + b_ref[...]


def _adaln(c, w, b):
    bsz, d = c.shape
    n = w.shape[1]
    tn = ADALN_COLS
    return pl.pallas_call(
        _adaln_kernel,
        grid=(n // tn,),
        in_specs=[pl.BlockSpec((bsz, d), lambda j: (0, 0)),
                  pl.BlockSpec((d, tn), lambda j: (0, j)),
                  pl.BlockSpec((1, tn), lambda j: (0, j))],
        out_specs=pl.BlockSpec((bsz, tn), lambda j: (0, j)),
        out_shape=jax.ShapeDtypeStruct((bsz, n), F32),
        compiler_params=_cparams("arbitrary"),
        name="adaln",
    )(c, w, b.reshape(1, n))


MIX_TILE = 512
MIX_CHUNK = 256
CONV_COLS = 256
SUBLANES = 8
MIXER_VMEM_LIMIT = 60 * 1024 * 1024


def _mixer_kernel(x_ref, sh_ref, sc_ref, gm_ref, g_ref, win_ref, wa_ref, wb_ref, bb_ref, lng_ref, lnb_ref,
                  hg_ref, grp_ref, grpt_ref, wout_ref, o_ref, ua_scr, ub_scr, rot_scr, zb_scr, h_scr):
    da = wa_ref.shape[1]
    rc = MIX_CHUNK
    n_rot = rc + HALO - SUBLANES

    @pl.when(pl.program_id(1) == 0)
    def _():
        ua_scr[0:HALO, :] = jnp.zeros((HALO, da), F32)
        ub_scr[0:HALO, :] = jnp.zeros((HALO, da), F32)

    x = x_ref[...]
    ms = jnp.mean(x * x, axis=-1, keepdims=True)
    h_scr[...] = (x * lax.rsqrt(ms + EPS) * g_ref[...] * (1.0 + sc_ref[0]) + sh_ref[0]).astype(BF16)

    for c in range(MIX_TILE // rc):
        rows = slice(c * rc, (c + 1) * rc)
        h = h_scr[rows, :]

        def proj(k):
            return _dot(h, win_ref[:, k * da:(k + 1) * da])

        ua_scr[HALO:HALO + rc, :] = proj(2) * proj(0)
        ub_scr[HALO:HALO + rc, :] = proj(3) * jax.nn.sigmoid(proj(4))

        conv_a = jnp.zeros((rc, da), F32)
        for k in range(K_SHORT):
            off = HALO - (K_SHORT - 1) + k
            conv_a = conv_a + wa_ref[k:k + 1, :] * ua_scr[off:off + rc, :]
        ya = proj(1) * conv_a

        for cc in range(da // CONV_COLS):
            cols = slice(cc * CONV_COLS, (cc + 1) * CONV_COLS)
            for r in range(1, SUBLANES):
                rot_scr[r - 1, :, :] = ub_scr[r:r + n_rot, cols]
            acc = jnp.zeros((rc, CONV_COLS), F32) + bb_ref[:, cols]
            for k in range(K_CONF):
                q, r = divmod(HALO - (K_CONF - 1) + k, SUBLANES)
                if r == 0:
                    src = ub_scr[q * SUBLANES:q * SUBLANES + rc, cols]
                else:
                    src = rot_scr[r - 1, q * SUBLANES:q * SUBLANES + rc, :]
                acc = acc + wb_ref[k:k + 1, cols] * src
            zb_scr[:, cols] = acc

        ua_scr[0:HALO, :] = ua_scr[rc:rc + HALO, :]
        ub_scr[0:HALO, :] = ub_scr[rc:rc + HALO, :]

        zb = zb_scr[...]
        mu = jnp.mean(zb, axis=-1, keepdims=True)
        zc = zb - mu
        var = jnp.mean(zc * zc, axis=-1, keepdims=True)
        zn = zc * lax.rsqrt(var + EPS) * lng_ref[...] + lnb_ref[...]
        zs = zn * jax.nn.sigmoid(zn)

        y = jnp.concatenate([ya, zs], axis=-1)
        gsum = _dot((y * y).astype(BF16), grp_ref[...])
        scale = lax.rsqrt(gsum * (1.0 / HEAD_CH) + EPS)
        scale_full = _dot(jnp.concatenate(_split_bf16(scale), axis=-1), grpt_ref[...])
        yn = (y * scale_full * hg_ref[...]).astype(BF16)
        o_ref[rows, :] = x_ref[rows, :] + gm_ref[0] * _dot(yn, wout_ref[...])


def _mixer(xf, sh, sc, g_m, norm_g, w_in_bf, conv_a_w, conv_b_w, conv_b_b, ln_g, ln_b, head_g, w_out_bf,
           seq, b0, nb):
    d = xf.shape[1]
    t = nb * seq
    da = conv_a_w.shape[1]
    ts = MIX_TILE
    per_seq = seq // ts
    n_heads = d // HEAD_CH
    head_of = jnp.arange(d, dtype=I32) // HEAD_CH
    grp = (head_of[:, None] == jnp.arange(n_heads, dtype=I32)[None, :]).astype(BF16)
    const = lambda shape: pl.BlockSpec(shape, lambda b, s: (0,) * len(shape))
    resident = lambda shape: pl.BlockSpec(shape, lambda b, s: (0,) * len(shape), pipeline_mode=pl.Buffered(1))
    vec = pl.BlockSpec((1, 1, d), lambda b, s: (b0 + b, 0, 0))
    n_rot = MIX_CHUNK + HALO - SUBLANES
    return pl.pallas_call(
        _mixer_kernel,
        grid=(nb, per_seq),
        in_specs=[
            pl.BlockSpec((ts, d), lambda b, s: ((b0 + b) * per_seq + s, 0)), vec, vec, vec, const((1, d)),
            resident(w_in_bf.shape),
            const((K_SHORT, da)), const((K_CONF, da)), const((1, da)), const((1, da)), const((1, da)),
            const((1, d)), const((d, n_heads)), const((2 * n_heads, d)), resident((d, d)),
        ],
        out_specs=pl.BlockSpec((ts, d), lambda b, s: (b * per_seq + s, 0)),
        out_shape=jax.ShapeDtypeStruct((t, d), F32),
        scratch_shapes=[pltpu.VMEM((HALO + MIX_CHUNK, da), F32), pltpu.VMEM((HALO + MIX_CHUNK, da), F32),
                        pltpu.VMEM((SUBLANES - 1, n_rot, CONV_COLS), F32), pltpu.VMEM((MIX_CHUNK, da), F32),
                        pltpu.VMEM((MIX_TILE, d), BF16)],
        compiler_params=_cparams("arbitrary", "arbitrary", vmem=MIXER_VMEM_LIMIT),
        name="mixer",
    )(xf, sh, sc, g_m, norm_g, w_in_bf, conv_a_w, conv_b_w, conv_b_b.reshape(1, da), ln_g.reshape(1, da),
      ln_b.reshape(1, da), head_g.reshape(1, d), grp, jnp.concatenate([grp.T, grp.T], axis=0), w_out_bf)


def _first_argmax(vals, iota, size):
    m = jnp.max(vals, axis=0, keepdims=True)
    idx = jnp.min(jnp.where(vals == m, iota, size), axis=0, keepdims=True)
    return m, idx


def _ffn_pre_kernel(x_ref, sh_ref, sc_ref, gf_ref, g_ref, wrh_ref, wrl_ref, rb_ref, tri_ref,
                    wsg_ref, wsu_ref, wsd_ref,
                    base_ref, hp_ref, idx_ref, rank_ref, wt_ref, cnt_ref, carry_scr):
    tm, d = x_ref.shape
    half = d // 2

    @pl.when(pl.program_id(0) == 0)
    def _():
        carry_scr[...] = jnp.zeros_like(carry_scr)

    x = x_ref[...]
    ms = jnp.mean(x * x, axis=-1, keepdims=True)
    h = x * lax.rsqrt(ms + EPS) * g_ref[...] * (1.0 + sc_ref[0]) + sh_ref[0]
    h_hi, h_lo = _split_bf16(h)
    hp_ref[...] = _pack_halves(h[:, :half], h[:, half:])

    nt = (((1,), (1,)), ((), ()))
    logits = (lax.dot_general(wrh_ref[...], h_hi, nt, preferred_element_type=F32)
              + lax.dot_general(wrh_ref[...], h_lo, nt, preferred_element_type=F32)
              + lax.dot_general(wrl_ref[...], h_hi, nt, preferred_element_type=F32))
    act = jax.nn.silu(_dot(h_hi, wsg_ref[...])) * _dot(h_hi, wsu_ref[...])
    base_ref[...] = x + gf_ref[0] * _dot(act.astype(BF16), wsd_ref[...])

    scores = jax.nn.sigmoid(logits)
    biased = scores + rb_ref[...]

    b3 = biased.reshape(N_GROUPS, GROUP_SIZE, tm)
    sub_iota = lax.broadcasted_iota(I32, (N_GROUPS, GROUP_SIZE, tm), 1)
    m1 = jnp.max(b3, axis=1, keepdims=True)
    i1 = jnp.min(jnp.where(b3 == m1, sub_iota, GROUP_SIZE), axis=1, keepdims=True)
    m2 = jnp.max(jnp.where(sub_iota == i1, -jnp.inf, b3), axis=1, keepdims=True)
    gscore = (m1 + m2).reshape(N_GROUPS, tm)

    g_iota = lax.broadcasted_iota(I32, (N_GROUPS, tm), 0)
    gsel = jnp.zeros((N_GROUPS, tm), jnp.bool_)
    gwork = gscore
    for _ in range(TOPK_GROUPS):
        _, gi = _first_argmax(gwork, g_iota, N_GROUPS)
        hit = g_iota == gi
        gsel = gsel | hit
        gwork = jnp.where(hit, -jnp.inf, gwork)
    emask = jnp.broadcast_to(gsel.reshape(N_GROUPS, 1, tm), (N_GROUPS, GROUP_SIZE, tm)).reshape(N_EXPERTS, tm)

    e_iota = lax.broadcasted_iota(I32, (N_EXPERTS, tm), 0)
    work = jnp.where(emask, biased, -jnp.inf)
    chosen = jnp.zeros((N_EXPERTS, tm), jnp.bool_)
    idx_rows, w_rows = [], []
    for _ in range(TOP_K):
        _, ei = _first_argmax(work, e_iota, N_EXPERTS)
        hit = e_iota == ei
        chosen = chosen | hit
        work = jnp.where(hit, -jnp.inf, work)
        idx_rows.append(ei)
        w_rows.append(jnp.sum(jnp.where(hit, scores, 0.0), axis=0, keepdims=True))
    idx = jnp.concatenate(idx_rows, axis=0)
    w = jnp.concatenate(w_rows, axis=0)
    w = w / jnp.sum(w, axis=0, keepdims=True) * ROUTED_SCALE

    onehot = chosen.astype(BF16)
    before = _dot(onehot, tri_ref[...]) + carry_scr[:, 0:1]
    rank_rows = [jnp.sum(jnp.where(e_iota == idx_rows[k], before, 0.0), axis=0, keepdims=True)
                 for k in range(TOP_K)]
    carry_new = carry_scr[...] + jnp.sum(chosen.astype(F32), axis=1, keepdims=True)
    carry_scr[...] = carry_new
    cnt_ref[...] = carry_new.astype(I32)
    idx_ref[...] = idx
    rank_ref[...] = jnp.concatenate(rank_rows, axis=0).astype(I32)
    w_pad = jnp.concatenate([w, jnp.zeros((LANES - TOP_K, tm), F32)], axis=0)
    wt_ref[...] = w_pad.T


def _ffn_pre(x1, sh, sc, gf, g, w_router, router_bias, wsg, wsu, wsd, seq, b0):
    t, d = x1.shape
    f = wsg.shape[1]
    tm = FFN_TILE
    per_seq = seq // tm
    wr_hi, wr_lo = _split_bf16(w_router.T)
    tri = (jnp.arange(tm, dtype=I32)[:, None] < jnp.arange(tm, dtype=I32)[None, :]).astype(BF16)
    vec = pl.BlockSpec((1, 1, d), lambda i: (b0 + i // per_seq, 0, 0))
    const = lambda shape: pl.BlockSpec(shape, lambda i: (0,) * len(shape))
    return pl.pallas_call(
        _ffn_pre_kernel,
        grid=(t // tm,),
        in_specs=[pl.BlockSpec((tm, d), lambda i: (i, 0)), vec, vec, vec, const((1, d)),
                  const((N_EXPERTS, d)), const((N_EXPERTS, d)), const((N_EXPERTS, 1)), const((tm, tm)),
                  const((d, f)), const((d, f)), const((f, d))],
        out_specs=[pl.BlockSpec((tm, d), lambda i: (i, 0)),
                   pl.BlockSpec((tm, d // 2), lambda i: (i, 0)),
                   pl.BlockSpec((TOP_K, tm), lambda i: (0, i)),
                   pl.BlockSpec((TOP_K, tm), lambda i: (0, i)),
                   pl.BlockSpec((tm, LANES), lambda i: (i, 0)),
                   const((N_EXPERTS, LANES))],
        out_shape=[jax.ShapeDtypeStruct((t, d), F32),
                   jax.ShapeDtypeStruct((t, d // 2), I32),
                   jax.ShapeDtypeStruct((TOP_K, t), I32),
                   jax.ShapeDtypeStruct((TOP_K, t), I32),
                   jax.ShapeDtypeStruct((t, LANES), F32),
                   jax.ShapeDtypeStruct((N_EXPERTS, LANES), I32)],
        scratch_shapes=[pltpu.VMEM((N_EXPERTS, LANES), F32)],
        compiler_params=_cparams("arbitrary"),
        name="ffn_pre",
    )(x1, sh, sc, gf, g, wr_hi, wr_lo, router_bias.reshape(N_EXPERTS, 1), tri,
      wsg.astype(BF16), wsu.astype(BF16), wsd.astype(BF16))


def _dest_kernel(pstart_ref, idx_ref, rank_ref, o_ref):
    idx = idx_ref[...]
    acc = rank_ref[...]
    for e in range(N_EXPERTS):
        acc = acc + jnp.where(idx == e, pstart_ref[e], 0)
    o_ref[...] = acc


def _dest(pstart, idx, rank):
    k, t = idx.shape
    tl = min(DEST_TILE, t)
    return pl.pallas_call(
        _dest_kernel,
        grid_spec=pltpu.PrefetchScalarGridSpec(
            num_scalar_prefetch=1,
            grid=(t // tl,),
            in_specs=[pl.BlockSpec((k, tl), lambda i, ps: (0, i)),
                      pl.BlockSpec((k, tl), lambda i, ps: (0, i))],
            out_specs=pl.BlockSpec((k, tl), lambda i, ps: (0, i))),
        out_shape=jax.ShapeDtypeStruct((k, t), I32),
        compiler_params=_cparams("arbitrary"),
        name="dest",
    )(pstart, idx, rank)


def _sc_mesh():
    return plsc.VectorSubcoreMesh(core_axis_name="c", subcore_axis_name="s")


def _sc_workers():
    info = plsc.get_sparse_core_info()
    return info.num_cores, info.num_cores * info.num_subcores


def _dispatch_rows(hp, dest3, n_rows):
    t, w = hp.shape
    n_chunks, top_k, chunk = dest3.shape
    n_cores, n_workers = _sc_workers()
    per_worker = n_chunks // n_workers

    @functools.partial(
        pl.kernel, mesh=_sc_mesh(),
        out_type=jax.ShapeDtypeStruct((n_rows, w), hp.dtype),
        scratch_types=[pltpu.VMEM((top_k, chunk), I32), pltpu.VMEM((chunk, w), hp.dtype),
                       pltpu.SemaphoreType.DMA],
        name="dispatch_rows")
    def k(hp_hbm, dest_hbm, xs_hbm, idx_v, rows_v, sem):
        wid = lax.axis_index("s") * n_cores + lax.axis_index("c")

        @pl.loop(0, per_worker)
        def _(j):
            c = wid * per_worker + j
            pltpu.sync_copy(dest_hbm.at[c], idx_v)
            pltpu.sync_copy(hp_hbm.at[pl.ds(c * chunk, chunk)], rows_v)
            copies = [pltpu.async_copy(rows_v, xs_hbm.at[idx_v.at[q]], sem) for q in range(top_k)]
            for cp in copies:
                cp.wait()

    return k(hp, dest3)


def _gather_rows(ys, flat_idx):
    n = flat_idx.shape[0]
    w = ys.shape[1]
    n_cores, n_workers = _sc_workers()
    per_worker = n // n_workers
    steps = per_worker // SC_CHUNK

    @functools.partial(
        pl.kernel, mesh=_sc_mesh(),
        out_type=jax.ShapeDtypeStruct((n, w), ys.dtype),
        scratch_types=[pltpu.VMEM((SC_CHUNK,), I32), pltpu.VMEM((SC_CHUNK, w), ys.dtype),
                       pltpu.SemaphoreType.DMA],
        name="gather_rows")
    def k(ys_hbm, idx_hbm, out_hbm, idx_v, rows_v, sem):
        wid = lax.axis_index("s") * n_cores + lax.axis_index("c")

        @pl.loop(0, steps)
        def _(j):
            base = wid * per_worker + j * SC_CHUNK
            pltpu.sync_copy(idx_hbm.at[pl.ds(base, SC_CHUNK)], idx_v)
            pltpu.async_copy(ys_hbm.at[idx_v], rows_v, sem).wait()
            pltpu.sync_copy(rows_v, out_hbm.at[pl.ds(base, SC_CHUNK)])

    return k(ys, flat_idx)


def _experts_kernel(be_ref, nxt_ref, nu_ref, x_ref, wg_hbm, wu_hbm, wd_hbm, o_ref,
                    wg_stage, wu_stage, wd_stage, wg_scr, wu_scr, wd_scr, sem):
    half = wg_scr.shape[0] // 2

    def weight_copies(expert):
        return (pltpu.make_async_copy(wg_hbm.at[expert], wg_stage, sem.at[0]),
                pltpu.make_async_copy(wu_hbm.at[expert], wu_stage, sem.at[1]),
                pltpu.make_async_copy(wd_hbm.at[expert], wd_stage, sem.at[2]))

    def one_block(blk, rows):
        e = be_ref[blk]
        fresh = jnp.logical_or(blk == 0, e != be_ref[jnp.maximum(blk - 1, 0)])

        @pl.when(blk == 0)
        def _():
            for cp in weight_copies(e):
                cp.start()

        @pl.when(fresh)
        def _():
            for cp in weight_copies(e):
                cp.wait()
            wg_scr[...] = wg_stage[...].astype(BF16)
            wu_scr[...] = wu_stage[...].astype(BF16)
            wd_scr[...] = wd_stage[...].astype(BF16)

        @pl.when(jnp.logical_and(fresh, nxt_ref[blk] >= 0))
        def _():
            for cp in weight_copies(nxt_ref[blk]):
                cp.start()

        @pl.when(blk < nu_ref[0])
        def _():
            lo, hi = _unpack_halves(x_ref[rows, :])
            lo = lo.astype(BF16)
            hi = hi.astype(BF16)
            g = _dot(lo, wg_scr[0:half, :]) + _dot(hi, wg_scr[half:, :])
            u = _dot(lo, wu_scr[0:half, :]) + _dot(hi, wu_scr[half:, :])
            act = (g * jax.nn.sigmoid(g) * u).astype(BF16)
            y = _dot(act, wd_scr[...])
            o_ref[rows, :] = _pack_halves(y[:, :half], y[:, half:])

        @pl.when(blk >= nu_ref[0])
        def _():
            o_ref[rows, :] = jnp.zeros((ROW_BLOCK, o_ref.shape[1]), o_ref.dtype)

    for sub in range(STEP_BLOCKS):
        one_block(pl.program_id(0) * STEP_BLOCKS + sub, slice(sub * ROW_BLOCK, (sub + 1) * ROW_BLOCK))


def _experts(block_e, next_e, n_used, xs, w_gate, w_up, w_down):
    n_rows, w = xs.shape
    _, d, f = w_gate.shape
    step_rows = STEP_BLOCKS * ROW_BLOCK
    anywhere = pl.BlockSpec(memory_space=pl.ANY)
    return pl.pallas_call(
        _experts_kernel,
        grid_spec=pltpu.PrefetchScalarGridSpec(
            num_scalar_prefetch=3,
            grid=(n_rows // step_rows,),
            in_specs=[pl.BlockSpec((step_rows, w), lambda i, be, nx, nu: (i, 0)), anywhere, anywhere, anywhere],
            out_specs=pl.BlockSpec((step_rows, w), lambda i, be, nx, nu: (i, 0)),
            scratch_shapes=[pltpu.VMEM((d, f), F32), pltpu.VMEM((d, f), F32), pltpu.VMEM((f, d), F32),
                            pltpu.VMEM((d, f), BF16), pltpu.VMEM((d, f), BF16), pltpu.VMEM((f, d), BF16),
                            pltpu.SemaphoreType.DMA((3,))]),
        out_shape=jax.ShapeDtypeStruct((n_rows, w), I32),
        compiler_params=_cparams("arbitrary"),
        name="experts",
    )(block_e, next_e, n_used, xs, w_gate, w_up, w_down)


def _final_kernel(base_ref, g_ref, wt_ref, gf_ref, ng_ref, *rest):
    o_ref = rest[-1]
    half = base_ref.shape[1] // 2
    wt = wt_ref[...]
    acc_lo = jnp.zeros((base_ref.shape[0], half), F32)
    acc_hi = jnp.zeros((base_ref.shape[0], half), F32)
    for k in range(TOP_K):
        lo, hi = _unpack_halves(g_ref[k])
        wk = wt[:, k:k + 1]
        acc_lo = acc_lo + wk * lo
        acc_hi = acc_hi + wk * hi
    gf = gf_ref[0]
    x_lo = base_ref[:, :half] + gf[:, :half] * acc_lo
    x_hi = base_ref[:, half:] + gf[:, half:] * acc_hi
    ms = (jnp.sum(x_lo * x_lo, axis=-1, keepdims=True)
          + jnp.sum(x_hi * x_hi, axis=-1, keepdims=True)) * (1.0 / (2 * half))
    inv = lax.rsqrt(ms + EPS)
    o_ref[:, :half] = x_lo * inv * ng_ref[:, :half]
    o_ref[:, half:] = x_hi * inv * ng_ref[:, half:]


def _final(base, gathered, wt, gf, norm_g, seq, b0, t_total, prev):
    t, d = base.shape
    tm = FINAL_TILE
    per_seq = seq // tm
    in_specs = [pl.BlockSpec((tm, d), lambda i: (i, 0)),
                pl.BlockSpec((TOP_K, tm, d // 2), lambda i: (0, i, 0)),
                pl.BlockSpec((tm, LANES), lambda i: (i, 0)),
                pl.BlockSpec((1, 1, d), lambda i: (b0 + i // per_seq, 0, 0)),
                pl.BlockSpec((1, d), lambda i: (0, 0))]
    args = [base, gathered, wt, gf, norm_g.reshape(1, d)]
    aliases = {}
    if prev is not None:
        in_specs.append(pl.BlockSpec(memory_space=pl.ANY))
        args.append(prev)
        aliases = {len(args) - 1: 0}
    return pl.pallas_call(
        _final_kernel,
        grid=(t // tm,),
        in_specs=in_specs,
        out_specs=pl.BlockSpec((tm, d), lambda i: (b0 * per_seq + i, 0)),
        out_shape=jax.ShapeDtypeStruct((t_total, d), F32),
        input_output_aliases=aliases,
        compiler_params=_cparams("arbitrary"),
        name="final",
    )(*args)


def _block_table(counts, n_blocks):
    padded = (counts + ROW_BLOCK - 1) // ROW_BLOCK * ROW_BLOCK
    pend = jnp.cumsum(padded)
    pstart = (pend - padded).astype(I32)
    n_used = (pend[-1] // ROW_BLOCK).astype(I32)
    blk = jnp.arange(n_blocks, dtype=I32)
    ended = (pend[None, :] <= (blk * ROW_BLOCK)[:, None]).astype(I32)
    block_e = jnp.minimum(jnp.sum(ended, axis=1), N_EXPERTS - 1).astype(I32)
    last_e = block_e[jnp.maximum(n_used - 1, 0)]
    block_e = jnp.where(blk < n_used, block_e, last_e)
    ex = jnp.arange(N_EXPERTS, dtype=I32)
    later = jnp.logical_and(ex[None, :] > ex[:, None], padded[None, :] > 0)
    nxt = jnp.min(jnp.where(later, ex[None, :], N_EXPERTS), axis=1)
    nxt = jnp.where(nxt == N_EXPERTS, -1, nxt).astype(I32)
    next_e = jnp.sum(jnp.where(block_e[:, None] == ex[None, :], nxt[None, :], 0), axis=1).astype(I32)
    return pstart, block_e, next_e, n_used.reshape(1)


def _group_sizes(bsz):
    if bsz % 8 == 0:
        return (5 * bsz // 8, 3 * bsz // 8)
    if bsz % 2 == 0:
        return (bsz // 2, bsz // 2)
    return (bsz,)


def _moe_group(out_prev, b0, nb, xf, mods, seq, p):
    sh_m, sc_m, g_m, sh_f, sc_f, g_f = mods
    d = xf.shape[1]
    t = nb * seq
    x1 = _mixer(xf, sh_m, sc_m, g_m, p["norm_mix_g"].reshape(1, d), p["w_in"], p["conv_a_w"], p["conv_b_w"],
                p["conv_b_b"], p["ln_b_g"], p["ln_b_b"], p["head_g"], p["w_out"], seq, b0, nb)
    base, hp, idx, rank, wt, counts = _ffn_pre(x1, sh_f, sc_f, g_f, p["norm_ffn_g"].reshape(1, d), p["w_router"],
                                               p["router_bias"], p["w_shared_gate"], p["w_shared_up"],
                                               p["w_shared_down"], seq, b0)
    n_assign = t * TOP_K
    n_blocks = (n_assign + N_EXPERTS * (ROW_BLOCK - 1) + ROW_BLOCK - 1) // ROW_BLOCK
    n_blocks = (n_blocks + STEP_BLOCKS - 1) // STEP_BLOCKS * STEP_BLOCKS
    pstart, block_e, next_e, n_used = _block_table(counts[:, 0], n_blocks)
    dest = _dest(pstart, idx, rank)
    dest3 = dest.reshape(TOP_K, t // SC_CHUNK, SC_CHUNK).transpose(1, 0, 2)
    xs = _dispatch_rows(hp, dest3, n_blocks * ROW_BLOCK)
    ys = _experts(block_e, next_e, n_used, xs, p["w_gate"], p["w_up"], p["w_down"])
    gathered = _gather_rows(ys, dest.reshape(-1)).reshape(TOP_K, t, d // 2)
    return _final(base, gathered, wt, g_f, p["norm_final_g"], seq, b0, xf.shape[0], out_prev)


def _layer(x, c, p):
    bsz, seq, d = x.shape
    xf = x.reshape(bsz * seq, d)
    mod = _adaln(c, p["w_ada"], p["b_ada"])
    mods = [m.reshape(bsz, 1, d) for m in jnp.split(mod, 6, axis=-1)]
    out, b0 = None, 0
    for nb in _group_sizes(bsz):
        out = _moe_group(out, b0, nb, xf, mods, seq, p)
        b0 += nb
    return out.reshape(bsz, seq, d)


def kernel(x, c, w_ada, b_ada, norm_mix_g, w_in, conv_a_w, conv_b_w, conv_b_b, ln_b_g, ln_b_b, head_norm_a_g,
           head_norm_b_g, w_out, norm_ffn_g, w_router, router_bias, w_gate, w_up, w_down, w_shared_gate,
           w_shared_up, w_shared_down, norm_final_g):
    assert w_ada.shape[0] == 1, "the fused final norm assumes a single layer"
    p = dict(w_ada=w_ada[0], b_ada=b_ada[0], norm_mix_g=norm_mix_g[0], w_in=w_in[0].astype(BF16),
             conv_a_w=conv_a_w[0], conv_b_w=conv_b_w[0], conv_b_b=conv_b_b[0], ln_b_g=ln_b_g[0], ln_b_b=ln_b_b[0],
             head_g=jnp.concatenate([head_norm_a_g[0], head_norm_b_g[0]]), w_out=w_out[0].astype(BF16),
             norm_ffn_g=norm_ffn_g[0], w_router=w_router[0], router_bias=router_bias[0], w_gate=w_gate[0],
             w_up=w_up[0], w_down=w_down[0], w_shared_gate=w_shared_gate[0], w_shared_up=w_shared_up[0],
             w_shared_down=w_shared_down[0], norm_final_g=norm_final_g)
    return _layer(x, c, p)
```

```python
import functools

import jax
import jax.numpy as jnp
from jax import lax
from jax.experimental import pallas as pl
from jax.experimental.pallas import tpu as pltpu
from jax.experimental.pallas import tpu_sc as plsc

F32 = jnp.float32
BF16 = jnp.bfloat16
I32 = jnp.int32

HEAD_CH = 64
K_SHORT = 3
K_CONF = 31
N_EXPERTS = 64
TOP_K = 8
N_GROUPS = 8
TOPK_GROUPS = 4
GROUP_SIZE = N_EXPERTS // N_GROUPS
ROUTED_SCALE = 2.5
EPS = 1e-6

LANES = 128
HALO = 32
ROW_BLOCK = 512
STEP_BLOCKS = 3
VMEM_LIMIT = 56 * 1024 * 1024
HI_MASK = -65536
SC_CHUNK = 32
ADALN_COLS = 1024
FFN_TILE = 512
DEST_TILE = 4096
FINAL_TILE = 512


def _cparams(*sem, vmem=VMEM_LIMIT):
    return pltpu.CompilerParams(dimension_semantics=sem, vmem_limit_bytes=vmem)


def _dot(a, b):
    return jnp.dot(a, b, preferred_element_type=F32)


def _split_bf16(x):
    hi = x.astype(BF16)
    lo = (x - hi.astype(F32)).astype(BF16)
    return hi, lo


def _pack_halves(lo, hi):
    lo_b = lax.bitcast_convert_type(lo.astype(BF16).astype(F32), I32)
    hi_b = lax.bitcast_convert_type(hi.astype(BF16).astype(F32), I32)
    return lax.shift_right_logical(lo_b, 16) | (hi_b & HI_MASK)


def _unpack_halves(p):
    lo = lax.bitcast_convert_type(lax.shift_left(p, 16), F32)
    hi = lax.bitcast_convert_type(p & HI_MASK, F32)
    return lo, hi


def _adaln_kernel(c_ref, w_ref, b_ref, o_ref):
    c = c_ref[...]
    ca = c * jax.nn.sigmoid(c)
    chi, clo = _split_bf16(ca)
    whi, wlo = _split_bf16(w_ref[...])
    o_ref[...] = _dot(chi, whi) + _dot(clo, whi) + _dot(chi, wlo) + b_ref[...]


def _adaln(c, w, b):
    bsz, d = c.shape
    n = w.shape[1]
    tn = ADALN_COLS
    return pl.pallas_call(
        _adaln_kernel,
        grid=(n // tn,),
        in_specs=[pl.BlockSpec((bsz, d), lambda j: (0, 0)),
                  pl.BlockSpec((d, tn), lambda j: (0, j)),
                  pl.BlockSpec((1, tn), lambda j: (0, j))],
        out_specs=pl.BlockSpec((bsz, tn), lambda j: (0, j)),
        out_shape=jax.ShapeDtypeStruct((bsz, n), F32),
        compiler_params=_cparams("arbitrary"),
        name="adaln",
    )(c, w, b.reshape(1, n))


MIX_TILE = 512
MIX_CHUNK = 256
CONV_COLS = 256
SUBLANES = 8
MIXER_VMEM_LIMIT = 60 * 1024 * 1024


def _mixer_kernel(x_ref, sh_ref, sc_ref, gm_ref, g_ref, win_ref, wa_ref, wb_ref, bb_ref, lng_ref, lnb_ref,
                  hg_ref, grp_ref, grpt_ref, wout_ref, o_ref, ua_scr, ub_scr, rot_scr, zb_scr, h_scr):
    da = wa_ref.shape[1]
    rc = MIX_CHUNK
    n_rot = rc + HALO - SUBLANES

    @pl.when(pl.program_id(1) == 0)
    def _():
        ua_scr[0:HALO, :] = jnp.zeros((HALO, da), F32)
        ub_scr[0:HALO, :] = jnp.zeros((HALO, da), F32)

    x = x_ref[...]
    ms = jnp.mean(x * x, axis=-1, keepdims=True)
    h_scr[...] = (x * lax.rsqrt(ms + EPS) * g_ref[...] * (1.0 + sc_ref[0]) + sh_ref[0]).astype(BF16)

    for c in range(MIX_TILE // rc):
        rows = slice(c * rc, (c + 1) * rc)
        h = h_scr[rows, :]

        def proj(k):
            return _dot(h, win_ref[:, k * da:(k + 1) * da])

        ua_scr[HALO:HALO + rc, :] = proj(2) * proj(0)
        ub_scr[HALO:HALO + rc, :] = proj(3) * jax.nn.sigmoid(proj(4))

        conv_a = jnp.zeros((rc, da), F32)
        for k in range(K_SHORT):
            off = HALO - (K_SHORT - 1) + k
            conv_a = conv_a + wa_ref[k:k + 1, :] * ua_scr[off:off + rc, :]
        ya = proj(1) * conv_a

        for cc in range(da // CONV_COLS):
            cols = slice(cc * CONV_COLS, (cc + 1) * CONV_COLS)
            for r in range(1, SUBLANES):
                rot_scr[r - 1, :, :] = ub_scr[r:r + n_rot, cols]
            acc = jnp.zeros((rc, CONV_COLS), F32) + bb_ref[:, cols]
            for k in range(K_CONF):
                q, r = divmod(HALO - (K_CONF - 1) + k, SUBLANES)
                if r == 0:
                    src = ub_scr[q * SUBLANES:q * SUBLANES + rc, cols]
                else:
                    src = rot_scr[r - 1, q * SUBLANES:q * SUBLANES + rc, :]
                acc = acc + wb_ref[k:k + 1, cols] * src
            zb_scr[:, cols] = acc

        ua_scr[0:HALO, :] = ua_scr[rc:rc + HALO, :]
        ub_scr[0:HALO, :] = ub_scr[rc:rc + HALO, :]

        zb = zb_scr[...]
        mu = jnp.mean(zb, axis=-1, keepdims=True)
        zc = zb - mu
        var = jnp.mean(zc * zc, axis=-1, keepdims=True)
        zn = zc * lax.rsqrt(var + EPS) * lng_ref[...] + lnb_ref[...]
        zs = zn * jax.nn.sigmoid(zn)

        y = jnp.concatenate([ya, zs], axis=-1)
        gsum = _dot((y * y).astype(BF16), grp_ref[...])
        scale = lax.rsqrt(gsum * (1.0 / HEAD_CH) + EPS)
        scale_full = _dot(jnp.concatenate(_split_bf16(scale), axis=-1), grpt_ref[...])
        yn = (y * scale_full * hg_ref[...]).astype(BF16)
        o_ref[rows, :] = x_ref[rows, :] + gm_ref[0] * _dot(yn, wout_ref[...])


def _mixer(xf, sh, sc, g_m, norm_g, w_in_bf, conv_a_w, conv_b_w, conv_b_b, ln_g, ln_b, head_g, w_out_bf,
           seq, b0, nb):
    d = xf.shape[1]
    t = nb * seq
    da = conv_a_w.shape[1]
    ts = MIX_TILE
    per_seq = seq // ts
    n_heads = d // HEAD_CH
    head_of = jnp.arange(d, dtype=I32) // HEAD_CH
    grp = (head_of[:, None] == jnp.arange(n_heads, dtype=I32)[None, :]).astype(BF16)
    const = lambda shape: pl.BlockSpec(shape, lambda b, s: (0,) * len(shape))
    resident = lambda shape: pl.BlockSpec(shape, lambda b, s: (0,) * len(shape), pipeline_mode=pl.Buffered(1))
    vec = pl.BlockSpec((1, 1, d), lambda b, s: (b0 + b, 0, 0))
    n_rot = MIX_CHUNK + HALO - SUBLANES
    return pl.pallas_call(
        _mixer_kernel,
        grid=(nb, per_seq),
        in_specs=[
            pl.BlockSpec((ts, d), lambda b, s: ((b0 + b) * per_seq + s, 0)), vec, vec, vec, const((1, d)),
            resident(w_in_bf.shape),
            const((K_SHORT, da)), const((K_CONF, da)), const((1, da)), const((1, da)), const((1, da)),
            const((1, d)), const((d, n_heads)), const((2 * n_heads, d)), resident((d, d)),
        ],
        out_specs=pl.BlockSpec((ts, d), lambda b, s: (b * per_seq + s, 0)),
        out_shape=jax.ShapeDtypeStruct((t, d), F32),
        scratch_shapes=[pltpu.VMEM((HALO + MIX_CHUNK, da), F32), pltpu.VMEM((HALO + MIX_CHUNK, da), F32),
                        pltpu.VMEM((SUBLANES - 1, n_rot, CONV_COLS), F32), pltpu.VMEM((MIX_CHUNK, da), F32),
                        pltpu.VMEM((MIX_TILE, d), BF16)],
        compiler_params=_cparams("arbitrary", "arbitrary", vmem=MIXER_VMEM_LIMIT),
        name="mixer",
    )(xf, sh, sc, g_m, norm_g, w_in_bf, conv_a_w, conv_b_w, conv_b_b.reshape(1, da), ln_g.reshape(1, da),
      ln_b.reshape(1, da), head_g.reshape(1, d), grp, jnp.concatenate([grp.T, grp.T], axis=0), w_out_bf)


def _first_argmax(vals, iota, size):
    m = jnp.max(vals, axis=0, keepdims=True)
    idx = jnp.min(jnp.where(vals == m, iota, size), axis=0, keepdims=True)
    return m, idx


def _ffn_pre_kernel(x_ref, sh_ref, sc_ref, gf_ref, g_ref, wrh_ref, wrl_ref, rb_ref, tri_ref,
                    wsg_ref, wsu_ref, wsd_ref,
                    base_ref, hp_ref, idx_ref, rank_ref, wt_ref, cnt_ref, carry_scr):
    tm, d = x_ref.shape
    half = d // 2

    @pl.when(pl.program_id(0) == 0)
    def _():
        carry_scr[...] = jnp.zeros_like(carry_scr)

    x = x_ref[...]
    ms = jnp.mean(x * x, axis=-1, keepdims=True)
    h = x * lax.rsqrt(ms + EPS) * g_ref[...] * (1.0 + sc_ref[0]) + sh_ref[0]
    h_hi, h_lo = _split_bf16(h)
    hp_ref[...] = _pack_halves(h[:, :half], h[:, half:])

    nt = (((1,), (1,)), ((), ()))
    logits = (lax.dot_general(wrh_ref[...], h_hi, nt, preferred_element_type=F32)
              + lax.dot_general(wrh_ref[...], h_lo, nt, preferred_element_type=F32)
              + lax.dot_general(wrl_ref[...], h_hi, nt, preferred_element_type=F32))
    act = jax.nn.silu(_dot(h_hi, wsg_ref[...])) * _dot(h_hi, wsu_ref[...])
    base_ref[...] = x + gf_ref[0] * _dot(act.astype(BF16), wsd_ref[...])

    scores = jax.nn.sigmoid(logits)
    biased = scores + rb_ref[...]

    b3 = biased.reshape(N_GROUPS, GROUP_SIZE, tm)
    sub_iota = lax.broadcasted_iota(I32, (N_GROUPS, GROUP_SIZE, tm), 1)
    m1 = jnp.max(b3, axis=1, keepdims=True)
    i1 = jnp.min(jnp.where(b3 == m1, sub_iota, GROUP_SIZE), axis=1, keepdims=True)
    m2 = jnp.max(jnp.where(sub_iota == i1, -jnp.inf, b3), axis=1, keepdims=True)
    gscore = (m1 + m2).reshape(N_GROUPS, tm)

    g_iota = lax.broadcasted_iota(I32, (N_GROUPS, tm), 0)
    gsel = jnp.zeros((N_GROUPS, tm), jnp.bool_)
    gwork = gscore
    for _ in range(TOPK_GROUPS):
        _, gi = _first_argmax(gwork, g_iota, N_GROUPS)
        hit = g_iota == gi
        gsel = gsel | hit
        gwork = jnp.where(hit, -jnp.inf, gwork)
    emask = jnp.broadcast_to(gsel.reshape(N_GROUPS, 1, tm), (N_GROUPS, GROUP_SIZE, tm)).reshape(N_EXPERTS, tm)

    e_iota = lax.broadcasted_iota(I32, (N_EXPERTS, tm), 0)
    work = jnp.where(emask, biased, -jnp.inf)
    chosen = jnp.zeros((N_EXPERTS, tm), jnp.bool_)
    idx_rows, w_rows = [], []
    for _ in range(TOP_K):
        _, ei = _first_argmax(work, e_iota, N_EXPERTS)
        hit = e_iota == ei
        chosen = chosen | hit
        work = jnp.where(hit, -jnp.inf, work)
        idx_rows.append(ei)
        w_rows.append(jnp.sum(jnp.where(hit, scores, 0.0), axis=0, keepdims=True))
    idx = jnp.concatenate(idx_rows, axis=0)
    w = jnp.concatenate(w_rows, axis=0)
    w = w / jnp.sum(w, axis=0, keepdims=True) * ROUTED_SCALE

    onehot = chosen.astype(BF16)
    before = _dot(onehot, tri_ref[...]) + carry_scr[:, 0:1]
    rank_rows = [jnp.sum(jnp.where(e_iota == idx_rows[k], before, 0.0), axis=0, keepdims=True)
                 for k in range(TOP_K)]
    carry_new = carry_scr[...] + jnp.sum(chosen.astype(F32), axis=1, keepdims=True)
    carry_scr[...] = carry_new
    cnt_ref[...] = carry_new.astype(I32)
    idx_ref[...] = idx
    rank_ref[...] = jnp.concatenate(rank_rows, axis=0).astype(I32)
    w_pad = jnp.concatenate([w, jnp.zeros((LANES - TOP_K, tm), F32)], axis=0)
    wt_ref[...] = w_pad.T


def _ffn_pre(x1, sh, sc, gf, g, w_router, router_bias, wsg, wsu, wsd, seq, b0):
    t, d = x1.shape
    f = wsg.shape[1]
    tm = FFN_TILE
    per_seq = seq // tm
    wr_hi, wr_lo = _split_bf16(w_router.T)
    tri = (jnp.arange(tm, dtype=I32)[:, None] < jnp.arange(tm, dtype=I32)[None, :]).astype(BF16)
    vec = pl.BlockSpec((1, 1, d), lambda i: (b0 + i // per_seq, 0, 0))
    const = lambda shape: pl.BlockSpec(shape, lambda i: (0,) * len(shape))
    return pl.pallas_call(
        _ffn_pre_kernel,
        grid=(t // tm,),
        in_specs=[pl.BlockSpec((tm, d), lambda i: (i, 0)), vec, vec, vec, const((1, d)),
                  const((N_EXPERTS, d)), const((N_EXPERTS, d)), const((N_EXPERTS, 1)), const((tm, tm)),
                  const((d, f)), const((d, f)), const((f, d))],
        out_specs=[pl.BlockSpec((tm, d), lambda i: (i, 0)),
                   pl.BlockSpec((tm, d // 2), lambda i: (i, 0)),
                   pl.BlockSpec((TOP_K, tm), lambda i: (0, i)),
                   pl.BlockSpec((TOP_K, tm), lambda i: (0, i)),
                   pl.BlockSpec((tm, LANES), lambda i: (i, 0)),
                   const((N_EXPERTS, LANES))],
        out_shape=[jax.ShapeDtypeStruct((t, d), F32),
                   jax.ShapeDtypeStruct((t, d // 2), I32),
                   jax.ShapeDtypeStruct((TOP_K, t), I32),
                   jax.ShapeDtypeStruct((TOP_K, t), I32),
                   jax.ShapeDtypeStruct((t, LANES), F32),
                   jax.ShapeDtypeStruct((N_EXPERTS, LANES), I32)],
        scratch_shapes=[pltpu.VMEM((N_EXPERTS, LANES), F32)],
        compiler_params=_cparams("arbitrary"),
        name="ffn_pre",
    )(x1, sh, sc, gf, g, wr_hi, wr_lo, router_bias.reshape(N_EXPERTS, 1), tri,
      wsg.astype(BF16), wsu.astype(BF16), wsd.astype(BF16))


def _dest_kernel(pstart_ref, idx_ref, rank_ref, o_ref):
    idx = idx_ref[...]
    acc = rank_ref[...]
    for e in range(N_EXPERTS):
        acc = acc + jnp.where(idx == e, pstart_ref[e], 0)
    o_ref[...] = acc


def _dest(pstart, idx, rank):
    k, t = idx.shape
    tl = min(DEST_TILE, t)
    return pl.pallas_call(
        _dest_kernel,
        grid_spec=pltpu.PrefetchScalarGridSpec(
            num_scalar_prefetch=1,
            grid=(t // tl,),
            in_specs=[pl.BlockSpec((k, tl), lambda i, ps: (0, i)),
                      pl.BlockSpec((k, tl), lambda i, ps: (0, i))],
            out_specs=pl.BlockSpec((k, tl), lambda i, ps: (0, i))),
        out_shape=jax.ShapeDtypeStruct((k, t), I32),
        compiler_params=_cparams("arbitrary"),
        name="dest",
    )(pstart, idx, rank)


def _sc_mesh():
    return plsc.VectorSubcoreMesh(core_axis_name="c", subcore_axis_name="s")


def _sc_workers():
    info = plsc.get_sparse_core_info()
    return info.num_cores, info.num_cores * info.num_subcores


def _dispatch_rows(hp, dest3, n_rows):
    t, w = hp.shape
    n_chunks, top_k, chunk = dest3.shape
    n_cores, n_workers = _sc_workers()
    per_worker = n_chunks // n_workers

    @functools.partial(
        pl.kernel, mesh=_sc_mesh(),
        out_type=jax.ShapeDtypeStruct((n_rows, w), hp.dtype),
        scratch_types=[pltpu.VMEM((top_k, chunk), I32), pltpu.VMEM((chunk, w), hp.dtype),
                       pltpu.SemaphoreType.DMA],
        name="dispatch_rows")
    def k(hp_hbm, dest_hbm, xs_hbm, idx_v, rows_v, sem):
        wid = lax.axis_index("s") * n_cores + lax.axis_index("c")

        @pl.loop(0, per_worker)
        def _(j):
            c = wid * per_worker + j
            pltpu.sync_copy(dest_hbm.at[c], idx_v)
            pltpu.sync_copy(hp_hbm.at[pl.ds(c * chunk, chunk)], rows_v)
            copies = [pltpu.async_copy(rows_v, xs_hbm.at[idx_v.at[q]], sem) for q in range(top_k)]
            for cp in copies:
                cp.wait()

    return k(hp, dest3)


def _gather_rows(ys, flat_idx):
    n = flat_idx.shape[0]
    w = ys.shape[1]
    n_cores, n_workers = _sc_workers()
    per_worker = n // n_workers
    steps = per_worker // SC_CHUNK

    @functools.partial(
        pl.kernel, mesh=_sc_mesh(),
        out_type=jax.ShapeDtypeStruct((n, w), ys.dtype),
        scratch_types=[pltpu.VMEM((SC_CHUNK,), I32), pltpu.VMEM((SC_CHUNK, w), ys.dtype),
                       pltpu.SemaphoreType.DMA],
        name="gather_rows")
    def k(ys_hbm, idx_hbm, out_hbm, idx_v, rows_v, sem):
        wid = lax.axis_index("s") * n_cores + lax.axis_index("c")

        @pl.loop(0, steps)
        def _(j):
            base = wid * per_worker + j * SC_CHUNK
            pltpu.sync_copy(idx_hbm.at[pl.ds(base, SC_CHUNK)], idx_v)
            pltpu.async_copy(ys_hbm.at[idx_v], rows_v, sem).wait()
            pltpu.sync_copy(rows_v, out_hbm.at[pl.ds(base, SC_CHUNK)])

    return k(ys, flat_idx)


def _experts_kernel(be_ref, nxt_ref, nu_ref, x_ref, wg_hbm, wu_hbm, wd_hbm, o_ref,
                    wg_stage, wu_stage, wd_stage, wg_scr, wu_scr, wd_scr, sem):
    half = wg_scr.shape[0] // 2

    def weight_copies(expert):
        return (pltpu.make_async_copy(wg_hbm.at[expert], wg_stage, sem.at[0]),
                pltpu.make_async_copy(wu_hbm.at[expert], wu_stage, sem.at[1]),
                pltpu.make_async_copy(wd_hbm.at[expert], wd_stage, sem.at[2]))

    def one_block(blk, rows):
        e = be_ref[blk]
        fresh = jnp.logical_or(blk == 0, e != be_ref[jnp.maximum(blk - 1, 0)])

        @pl.when(blk == 0)
        def _():
            for cp in weight_copies(e):
                cp.start()

        @pl.when(fresh)
        def _():
            for cp in weight_copies(e):
                cp.wait()
            wg_scr[...] = wg_stage[...].astype(BF16)
            wu_scr[...] = wu_stage[...].astype(BF16)
            wd_scr[...] = wd_stage[...].astype(BF16)

        @pl.when(jnp.logical_and(fresh, nxt_ref[blk] >= 0))
        def _():
            for cp in weight_copies(nxt_ref[blk]):
                cp.start()

        @pl.when(blk < nu_ref[0])
        def _():
            lo, hi = _unpack_halves(x_ref[rows, :])
            lo = lo.astype(BF16)
            hi = hi.astype(BF16)
            g = _dot(lo, wg_scr[0:half, :]) + _dot(hi, wg_scr[half:, :])
            u = _dot(lo, wu_scr[0:half, :]) + _dot(hi, wu_scr[half:, :])
            act = (g * jax.nn.sigmoid(g) * u).astype(BF16)
            y = _dot(act, wd_scr[...])
            o_ref[rows, :] = _pack_halves(y[:, :half], y[:, half:])

        @pl.when(blk >= nu_ref[0])
        def _():
            o_ref[rows, :] = jnp.zeros((ROW_BLOCK, o_ref.shape[1]), o_ref.dtype)

    for sub in range(STEP_BLOCKS):
        one_block(pl.program_id(0) * STEP_BLOCKS + sub, slice(sub * ROW_BLOCK, (sub + 1) * ROW_BLOCK))


def _experts(block_e, next_e, n_used, xs, w_gate, w_up, w_down):
    n_rows, w = xs.shape
    _, d, f = w_gate.shape
    step_rows = STEP_BLOCKS * ROW_BLOCK
    anywhere = pl.BlockSpec(memory_space=pl.ANY)
    return pl.pallas_call(
        _experts_kernel,
        grid_spec=pltpu.PrefetchScalarGridSpec(
            num_scalar_prefetch=3,
            grid=(n_rows // step_rows,),
            in_specs=[pl.BlockSpec((step_rows, w), lambda i, be, nx, nu: (i, 0)), anywhere, anywhere, anywhere],
            out_specs=pl.BlockSpec((step_rows, w), lambda i, be, nx, nu: (i, 0)),
            scratch_shapes=[pltpu.VMEM((d, f), F32), pltpu.VMEM((d, f), F32), pltpu.VMEM((f, d), F32),
                            pltpu.VMEM((d, f), BF16), pltpu.VMEM((d, f), BF16), pltpu.VMEM((f, d), BF16),
                            pltpu.SemaphoreType.DMA((3,))]),
        out_shape=jax.ShapeDtypeStruct((n_rows, w), I32),
        compiler_params=_cparams("arbitrary"),
        name="experts",
    )(block_e, next_e, n_used, xs, w_gate, w_up, w_down)


def _final_kernel(base_ref, g_ref, wt_ref, gf_ref, ng_ref, *rest):
    o_ref = rest[-1]
    half = base_ref.shape[1] // 2
    wt = wt_ref[...]
    acc_lo = jnp.zeros((base_ref.shape[0], half), F32)
    acc_hi = jnp.zeros((base_ref.shape[0], half), F32)
    for k in range(TOP_K):
        lo, hi = _unpack_halves(g_ref[k])
        wk = wt[:, k:k + 1]
        acc_lo = acc_lo + wk * lo
        acc_hi = acc_hi + wk * hi
    gf = gf_ref[0]
    x_lo = base_ref[:, :half] + gf[:, :half] * acc_lo
    x_hi = base_ref[:, half:] + gf[:, half:] * acc_hi
    ms = (jnp.sum(x_lo * x_lo, axis=-1, keepdims=True)
          + jnp.sum(x_hi * x_hi, axis=-1, keepdims=True)) * (1.0 / (2 * half))
    inv = lax.rsqrt(ms + EPS)
    o_ref[:, :half] = x_lo * inv * ng_ref[:, :half]
    o_ref[:, half:] = x_hi * inv * ng_ref[:, half:]


def _final(base, gathered, wt, gf, norm_g, seq, b0, t_total, prev):
    t, d = base.shape
    tm = FINAL_TILE
    per_seq = seq // tm
    in_specs = [pl.BlockSpec((tm, d), lambda i: (i, 0)),
                pl.BlockSpec((TOP_K, tm, d // 2), lambda i: (0, i, 0)),
                pl.BlockSpec((tm, LANES), lambda i: (i, 0)),
                pl.BlockSpec((1, 1, d), lambda i: (b0 + i // per_seq, 0, 0)),
                pl.BlockSpec((1, d), lambda i: (0, 0))]
    args = [base, gathered, wt, gf, norm_g.reshape(1, d)]
    aliases = {}
    if prev is not None:
        in_specs.append(pl.BlockSpec(memory_space=pl.ANY))
        args.append(prev)
        aliases = {len(args) - 1: 0}
    return pl.pallas_call(
        _final_kernel,
        grid=(t // tm,),
        in_specs=in_specs,
        out_specs=pl.BlockSpec((tm, d), lambda i: (b0 * per_seq + i, 0)),
        out_shape=jax.ShapeDtypeStruct((t_total, d), F32),
        input_output_aliases=aliases,
        compiler_params=_cparams("arbitrary"),
        name="final",
    )(*args)


def _block_table(counts, n_blocks):
    padded = (counts + ROW_BLOCK - 1) // ROW_BLOCK * ROW_BLOCK
    pend = jnp.cumsum(padded)
    pstart = (pend - padded).astype(I32)
    n_used = (pend[-1] // ROW_BLOCK).astype(I32)
    blk = jnp.arange(n_blocks, dtype=I32)
    ended = (pend[None, :] <= (blk * ROW_BLOCK)[:, None]).astype(I32)
    block_e = jnp.minimum(jnp.sum(ended, axis=1), N_EXPERTS - 1).astype(I32)
    last_e = block_e[jnp.maximum(n_used - 1, 0)]
    block_e = jnp.where(blk < n_used, block_e, last_e)
    ex = jnp.arange(N_EXPERTS, dtype=I32)
    later = jnp.logical_and(ex[None, :] > ex[:, None], padded[None, :] > 0)
    nxt = jnp.min(jnp.where(later, ex[None, :], N_EXPERTS), axis=1)
    nxt = jnp.where(nxt == N_EXPERTS, -1, nxt).astype(I32)
    next_e = jnp.sum(jnp.where(block_e[:, None] == ex[None, :], nxt[None, :], 0), axis=1).astype(I32)
    return pstart, block_e, next_e, n_used.reshape(1)


def _group_sizes(bsz):
    if bsz % 8 == 0:
        return (5 * bsz // 8, 3 * bsz // 8)
    if bsz % 2 == 0:
        return (bsz // 2, bsz // 2)
    return (bsz,)


def _moe_group(out_prev, b0, nb, xf, mods, seq, p):
    sh_m, sc_m, g_m, sh_f, sc_f, g_f = mods
    d = xf.shape[1]
    t = nb * seq
    x1 = _mixer(xf, sh_m, sc_m, g_m, p["norm_mix_g"].reshape(1, d), p["w_in"], p["conv_a_w"], p["conv_b_w"],
                p["conv_b_b"], p["ln_b_g"], p["ln_b_b"], p["head_g"], p["w_out"], seq, b0, nb)
    base, hp, idx, rank, wt, counts = _ffn_pre(x1, sh_f, sc_f, g_f, p["norm_ffn_g"].reshape(1, d), p["w_router"],
                                               p["router_bias"], p["w_shared_gate"], p["w_shared_up"],
                                               p["w_shared_down"], seq, b0)
    n_assign = t * TOP_K
    n_blocks = (n_assign + N_EXPERTS * (ROW_BLOCK - 1) + ROW_BLOCK - 1) // ROW_BLOCK
    n_blocks = (n_blocks + STEP_BLOCKS - 1) // STEP_BLOCKS * STEP_BLOCKS
    pstart, block_e, next_e, n_used = _block_table(counts[:, 0], n_blocks)
    dest = _dest(pstart, idx, rank)
    dest3 = dest.reshape(TOP_K, t // SC_CHUNK, SC_CHUNK).transpose(1, 0, 2)
    xs = _dispatch_rows(hp, dest3, n_blocks * ROW_BLOCK)
    ys = _experts(block_e, next_e, n_used, xs, p["w_gate"], p["w_up"], p["w_down"])
    gathered = _gather_rows(ys, dest.reshape(-1)).reshape(TOP_K, t, d // 2)
    return _final(base, gathered, wt, g_f, p["norm_final_g"], seq, b0, xf.shape[0], out_prev)


def _layer(x, c, p):
    bsz, seq, d = x.shape
    xf = x.reshape(bsz * seq, d)
    mod = _adaln(c, p["w_ada"], p["b_ada"])
    mods = [m.reshape(bsz, 1, d) for m in jnp.split(mod, 6, axis=-1)]
    out, b0 = None, 0
    for nb in _group_sizes(bsz):
        out = _moe_group(out, b0, nb, xf, mods, seq, p)
        b0 += nb
    return out.reshape(bsz, seq, d)


def kernel(x, c, w_ada, b_ada, norm_mix_g, w_in, conv_a_w, conv_b_w, conv_b_b, ln_b_g, ln_b_b, head_norm_a_g,
           head_norm_b_g, w_out, norm_ffn_g, w_router, router_bias, w_gate, w_up, w_down, w_shared_gate,
           w_shared_up, w_shared_down, norm_final_g):
    assert w_ada.shape[0] == 1, "the fused final norm assumes a single layer"
    p = dict(w_ada=w_ada[0], b_ada=b_ada[0], norm_mix_g=norm_mix_g[0], w_in=w_in[0].astype(BF16),
             conv_a_w=conv_a_w[0], conv_b_w=conv_b_w[0], conv_b_b=conv_b_b[0], ln_b_g=ln_b_g[0], ln_b_b=ln_b_b[0],
             head_g=jnp.concatenate([head_norm_a_g[0], head_norm_b_g[0]]), w_out=w_out[0].astype(BF16),
             norm_ffn_g=norm_ffn_g[0], w_router=w_router[0], router_bias=router_bias[0], w_gate=w_gate[0],
             w_up=w_up[0], w_down=w_down[0], w_shared_gate=w_shared_gate[0], w_shared_up=w_shared_up[0],
             w_shared_down=w_shared_down[0], norm_final_g=norm_final_g)
    return _layer(x, c, p)
```

```python
import functools

import jax
import jax.numpy as jnp
from jax import lax
from jax.experimental import pallas as pl
from jax.experimental.pallas import tpu as pltpu
from jax.experimental.pallas import tpu_sc as plsc

F32 = jnp.float32
BF16 = jnp.bfloat16
I32 = jnp.int32

HEAD_CH = 64
K_SHORT = 3
K_CONF = 31
N_EXPERTS = 64
TOP_K = 8
N_GROUPS = 8
TOPK_GROUPS = 4
GROUP_SIZE = N_EXPERTS // N_GROUPS
ROUTED_SCALE = 2.5
EPS = 1e-6

LANES = 128
HALO = 32
ROW_BLOCK = 512
STEP_BLOCKS = 4
VMEM_LIMIT = 56 * 1024 * 1024
HI_MASK = -65536
SC_CHUNK = 32
ADALN_COLS = 1024
FFN_TILE = 512
DEST_TILE = 4096
FINAL_TILE = 512


def _cparams(*sem, vmem=VMEM_LIMIT):
    return pltpu.CompilerParams(dimension_semantics=sem, vmem_limit_bytes=vmem)


def _dot(a, b):
    return jnp.dot(a, b, preferred_element_type=F32)


def _split_bf16(x):
    hi = x.astype(BF16)
    lo = (x - hi.astype(F32)).astype(BF16)
    return hi, lo


def _pack_halves(lo, hi):
    lo_b = lax.bitcast_convert_type(lo.astype(BF16).astype(F32), I32)
    hi_b = lax.bitcast_convert_type(hi.astype(BF16).astype(F32), I32)
    return lax.shift_right_logical(lo_b, 16) | (hi_b & HI_MASK)


def _unpack_halves(p):
    lo = lax.bitcast_convert_type(lax.shift_left(p, 16), F32)
    hi = lax.bitcast_convert_type(p & HI_MASK, F32)
    return lo, hi


def _adaln_kernel(c_ref, w_ref, b_ref, o_ref):
    c = c_ref[...]
    ca = c * jax.nn.sigmoid(c)
    chi, clo = _split_bf16(ca)
    whi, wlo = _split_bf16(w_ref[...])
    o_ref[...] = _dot(chi, whi) + _dot(clo, whi) + _dot(chi, wlo) + b_ref[...]


def _adaln(c, w, b):
    bsz, d = c.shape
    n = w.shape[1]
    tn = ADALN_COLS
    return pl.pallas_call(
        _adaln_kernel,
        grid=(n // tn,),
        in_specs=[pl.BlockSpec((bsz, d), lambda j: (0, 0)),
                  pl.BlockSpec((d, tn), lambda j: (0, j)),
                  pl.BlockSpec((1, tn), lambda j: (0, j))],
        out_specs=pl.BlockSpec((bsz, tn), lambda j: (0, j)),
        out_shape=jax.ShapeDtypeStruct((bsz, n), F32),
        compiler_params=_cparams("arbitrary"),
        name="adaln",
    )(c, w, b.reshape(1, n))


MIX_TILE = 512
MIX_CHUNK = 256
CONV_COLS = 256
SUBLANES = 8
MIXER_VMEM_LIMIT = 60 * 1024 * 1024


def _mixer_kernel(x_ref, sh_ref, sc_ref, gm_ref, g_ref, win_ref, wa_ref, wb_ref, bb_ref, lng_ref, lnb_ref,
                  hg_ref, grp_ref, grpt_ref, wout_ref, o_ref, ua_scr, ub_scr, rot_scr, zb_scr, h_scr):
    da = wa_ref.shape[1]
    rc = MIX_CHUNK
    n_rot = rc + HALO - SUBLANES

    @pl.when(pl.program_id(1) == 0)
    def _():
        ua_scr[0:HALO, :] = jnp.zeros((HALO, da), F32)
        ub_scr[0:HALO, :] = jnp.zeros((HALO, da), F32)

    x = x_ref[...]
    ms = jnp.mean(x * x, axis=-1, keepdims=True)
    h_scr[...] = (x * lax.rsqrt(ms + EPS) * g_ref[...] * (1.0 + sc_ref[0]) + sh_ref[0]).astype(BF16)

    for c in range(MIX_TILE // rc):
        rows = slice(c * rc, (c + 1) * rc)
        h = h_scr[rows, :]

        def proj(k):
            return _dot(h, win_ref[:, k * da:(k + 1) * da])

        ua_scr[HALO:HALO + rc, :] = proj(2) * proj(0)
        ub_scr[HALO:HALO + rc, :] = proj(3) * jax.nn.sigmoid(proj(4))

        conv_a = jnp.zeros((rc, da), F32)
        for k in range(K_SHORT):
            off = HALO - (K_SHORT - 1) + k
            conv_a = conv_a + wa_ref[k:k + 1, :] * ua_scr[off:off + rc, :]
        ya = proj(1) * conv_a

        for cc in range(da // CONV_COLS):
            cols = slice(cc * CONV_COLS, (cc + 1) * CONV_COLS)
            for r in range(1, SUBLANES):
                rot_scr[r - 1, :, :] = ub_scr[r:r + n_rot, cols]
            acc = jnp.zeros((rc, CONV_COLS), F32) + bb_ref[:, cols]
            for k in range(K_CONF):
                q, r = divmod(HALO - (K_CONF - 1) + k, SUBLANES)
                if r == 0:
                    src = ub_scr[q * SUBLANES:q * SUBLANES + rc, cols]
                else:
                    src = rot_scr[r - 1, q * SUBLANES:q * SUBLANES + rc, :]
                acc = acc + wb_ref[k:k + 1, cols] * src
            zb_scr[:, cols] = acc

        ua_scr[0:HALO, :] = ua_scr[rc:rc + HALO, :]
        ub_scr[0:HALO, :] = ub_scr[rc:rc + HALO, :]

        zb = zb_scr[...]
        mu = jnp.mean(zb, axis=-1, keepdims=True)
        zc = zb - mu
        var = jnp.mean(zc * zc, axis=-1, keepdims=True)
        zn = zc * lax.rsqrt(var + EPS) * lng_ref[...] + lnb_ref[...]
        zs = zn * jax.nn.sigmoid(zn)

        y = jnp.concatenate([ya, zs], axis=-1)
        gsum = _dot((y * y).astype(BF16), grp_ref[...])
        scale = lax.rsqrt(gsum * (1.0 / HEAD_CH) + EPS)
        scale_full = _dot(jnp.concatenate(_split_bf16(scale), axis=-1), grpt_ref[...])
        yn = (y * scale_full * hg_ref[...]).astype(BF16)
        o_ref[rows, :] = x_ref[rows, :] + gm_ref[0] * _dot(yn, wout_ref[...])


def _mixer(xf, sh, sc, g_m, norm_g, w_in_bf, conv_a_w, conv_b_w, conv_b_b, ln_g, ln_b, head_g, w_out_bf,
           seq, b0, nb):
    d = xf.shape[1]
    t = nb * seq
    da = conv_a_w.shape[1]
    ts = MIX_TILE
    per_seq = seq // ts
    n_heads = d // HEAD_CH
    head_of = jnp.arange(d, dtype=I32) // HEAD_CH
    grp = (head_of[:, None] == jnp.arange(n_heads, dtype=I32)[None, :]).astype(BF16)
    const = lambda shape: pl.BlockSpec(shape, lambda b, s: (0,) * len(shape))
    resident = lambda shape: pl.BlockSpec(shape, lambda b, s: (0,) * len(shape), pipeline_mode=pl.Buffered(1))
    vec = pl.BlockSpec((1, 1, d), lambda b, s: (b0 + b, 0, 0))
    n_rot = MIX_CHUNK + HALO - SUBLANES
    return pl.pallas_call(
        _mixer_kernel,
        grid=(nb, per_seq),
        in_specs=[
            pl.BlockSpec((ts, d), lambda b, s: ((b0 + b) * per_seq + s, 0)), vec, vec, vec, const((1, d)),
            resident(w_in_bf.shape),
            const((K_SHORT, da)), const((K_CONF, da)), const((1, da)), const((1, da)), const((1, da)),
            const((1, d)), const((d, n_heads)), const((2 * n_heads, d)), resident((d, d)),
        ],
        out_specs=pl.BlockSpec((ts, d), lambda b, s: (b * per_seq + s, 0)),
        out_shape=jax.ShapeDtypeStruct((t, d), F32),
        scratch_shapes=[pltpu.VMEM((HALO + MIX_CHUNK, da), F32), pltpu.VMEM((HALO + MIX_CHUNK, da), F32),
                        pltpu.VMEM((SUBLANES - 1, n_rot, CONV_COLS), F32), pltpu.VMEM((MIX_CHUNK, da), F32),
                        pltpu.VMEM((MIX_TILE, d), BF16)],
        compiler_params=_cparams("arbitrary", "arbitrary", vmem=MIXER_VMEM_LIMIT),
        name="mixer",
    )(xf, sh, sc, g_m, norm_g, w_in_bf, conv_a_w, conv_b_w, conv_b_b.reshape(1, da), ln_g.reshape(1, da),
      ln_b.reshape(1, da), head_g.reshape(1, d), grp, jnp.concatenate([grp.T, grp.T], axis=0), w_out_bf)


def _first_argmax(vals, iota, size):
    m = jnp.max(vals, axis=0, keepdims=True)
    idx = jnp.min(jnp.where(vals == m, iota, size), axis=0, keepdims=True)
    return m, idx


def _ffn_pre_kernel(x_ref, sh_ref, sc_ref, gf_ref, g_ref, wrh_ref, wrl_ref, rb_ref, tri_ref,
                    wsg_ref, wsu_ref, wsd_ref,
                    base_ref, hp_ref, idx_ref, rank_ref, wt_ref, cnt_ref, carry_scr):
    tm, d = x_ref.shape
    half = d // 2

    @pl.when(pl.program_id(0) == 0)
    def _():
        carry_scr[...] = jnp.zeros_like(carry_scr)

    x = x_ref[...]
    ms = jnp.mean(x * x, axis=-1, keepdims=True)
    h = x * lax.rsqrt(ms + EPS) * g_ref[...] * (1.0 + sc_ref[0]) + sh_ref[0]
    h_hi, h_lo = _split_bf16(h)
    hp_ref[...] = _pack_halves(h[:, :half], h[:, half:])

    nt = (((1,), (1,)), ((), ()))
    logits = (lax.dot_general(wrh_ref[...], h_hi, nt, preferred_element_type=F32)
              + lax.dot_general(wrh_ref[...], h_lo, nt, preferred_element_type=F32)
              + lax.dot_general(wrl_ref[...], h_hi, nt, preferred_element_type=F32))
    act = jax.nn.silu(_dot(h_hi, wsg_ref[...])) * _dot(h_hi, wsu_ref[...])
    base_ref[...] = x + gf_ref[0] * _dot(act.astype(BF16), wsd_ref[...])

    scores = jax.nn.sigmoid(logits)
    biased = scores + rb_ref[...]

    b3 = biased.reshape(N_GROUPS, GROUP_SIZE, tm)
    sub_iota = lax.broadcasted_iota(I32, (N_GROUPS, GROUP_SIZE, tm), 1)
    m1 = jnp.max(b3, axis=1, keepdims=True)
    i1 = jnp.min(jnp.where(b3 == m1, sub_iota, GROUP_SIZE), axis=1, keepdims=True)
    m2 = jnp.max(jnp.where(sub_iota == i1, -jnp.inf, b3), axis=1, keepdims=True)
    gscore = (m1 + m2).reshape(N_GROUPS, tm)

    g_iota = lax.broadcasted_iota(I32, (N_GROUPS, tm), 0)
    gsel = jnp.zeros((N_GROUPS, tm), jnp.bool_)
    gwork = gscore
    for _ in range(TOPK_GROUPS):
        _, gi = _first_argmax(gwork, g_iota, N_GROUPS)
        hit = g_iota == gi
        gsel = gsel | hit
        gwork = jnp.where(hit, -jnp.inf, gwork)
    emask = jnp.broadcast_to(gsel.reshape(N_GROUPS, 1, tm), (N_GROUPS, GROUP_SIZE, tm)).reshape(N_EXPERTS, tm)

    e_iota = lax.broadcasted_iota(I32, (N_EXPERTS, tm), 0)
    work = jnp.where(emask, biased, -jnp.inf)
    chosen = jnp.zeros((N_EXPERTS, tm), jnp.bool_)
    idx_rows, w_rows = [], []
    for _ in range(TOP_K):
        _, ei = _first_argmax(work, e_iota, N_EXPERTS)
        hit = e_iota == ei
        chosen = chosen | hit
        work = jnp.where(hit, -jnp.inf, work)
        idx_rows.append(ei)
        w_rows.append(jnp.sum(jnp.where(hit, scores, 0.0), axis=0, keepdims=True))
    idx = jnp.concatenate(idx_rows, axis=0)
    w = jnp.concatenate(w_rows, axis=0)
    w = w / jnp.sum(w, axis=0, keepdims=True) * ROUTED_SCALE

    onehot = chosen.astype(BF16)
    before = _dot(onehot, tri_ref[...]) + carry_scr[:, 0:1]
    rank_rows = [jnp.sum(jnp.where(e_iota == idx_rows[k], before, 0.0), axis=0, keepdims=True)
                 for k in range(TOP_K)]
    carry_new = carry_scr[...] + jnp.sum(chosen.astype(F32), axis=1, keepdims=True)
    carry_scr[...] = carry_new
    cnt_ref[...] = carry_new.astype(I32)
    idx_ref[...] = idx
    rank_ref[...] = jnp.concatenate(rank_rows, axis=0).astype(I32)
    w_pad = jnp.concatenate([w, jnp.zeros((LANES - TOP_K, tm), F32)], axis=0)
    wt_ref[...] = w_pad.T


def _ffn_pre(x1, sh, sc, gf, g, w_router, router_bias, wsg, wsu, wsd, seq, b0):
    t, d = x1.shape
    f = wsg.shape[1]
    tm = FFN_TILE
    per_seq = seq // tm
    wr_hi, wr_lo = _split_bf16(w_router.T)
    tri = (jnp.arange(tm, dtype=I32)[:, None] < jnp.arange(tm, dtype=I32)[None, :]).astype(BF16)
    vec = pl.BlockSpec((1, 1, d), lambda i: (b0 + i // per_seq, 0, 0))
    const = lambda shape: pl.BlockSpec(shape, lambda i: (0,) * len(shape))
    return pl.pallas_call(
        _ffn_pre_kernel,
        grid=(t // tm,),
        in_specs=[pl.BlockSpec((tm, d), lambda i: (i, 0)), vec, vec, vec, const((1, d)),
                  const((N_EXPERTS, d)), const((N_EXPERTS, d)), const((N_EXPERTS, 1)), const((tm, tm)),
                  const((d, f)), const((d, f)), const((f, d))],
        out_specs=[pl.BlockSpec((tm, d), lambda i: (i, 0)),
                   pl.BlockSpec((tm, d // 2), lambda i: (i, 0)),
                   pl.BlockSpec((TOP_K, tm), lambda i: (0, i)),
                   pl.BlockSpec((TOP_K, tm), lambda i: (0, i)),
                   pl.BlockSpec((tm, LANES), lambda i: (i, 0)),
                   const((N_EXPERTS, LANES))],
        out_shape=[jax.ShapeDtypeStruct((t, d), F32),
                   jax.ShapeDtypeStruct((t, d // 2), I32),
                   jax.ShapeDtypeStruct((TOP_K, t), I32),
                   jax.ShapeDtypeStruct((TOP_K, t), I32),
                   jax.ShapeDtypeStruct((t, LANES), F32),
                   jax.ShapeDtypeStruct((N_EXPERTS, LANES), I32)],
        scratch_shapes=[pltpu.VMEM((N_EXPERTS, LANES), F32)],
        compiler_params=_cparams("arbitrary"),
        name="ffn_pre",
    )(x1, sh, sc, gf, g, wr_hi, wr_lo, router_bias.reshape(N_EXPERTS, 1), tri,
      wsg.astype(BF16), wsu.astype(BF16), wsd.astype(BF16))


def _dest_kernel(pstart_ref, idx_ref, rank_ref, o_ref):
    idx = idx_ref[...]
    acc = rank_ref[...]
    for e in range(N_EXPERTS):
        acc = acc + jnp.where(idx == e, pstart_ref[e], 0)
    o_ref[...] = acc


def _dest(pstart, idx, rank):
    k, t = idx.shape
    tl = min(DEST_TILE, t)
    return pl.pallas_call(
        _dest_kernel,
        grid_spec=pltpu.PrefetchScalarGridSpec(
            num_scalar_prefetch=1,
            grid=(t // tl,),
            in_specs=[pl.BlockSpec((k, tl), lambda i, ps: (0, i)),
                      pl.BlockSpec((k, tl), lambda i, ps: (0, i))],
            out_specs=pl.BlockSpec((k, tl), lambda i, ps: (0, i))),
        out_shape=jax.ShapeDtypeStruct((k, t), I32),
        compiler_params=_cparams("arbitrary"),
        name="dest",
    )(pstart, idx, rank)


def _sc_mesh():
    return plsc.VectorSubcoreMesh(core_axis_name="c", subcore_axis_name="s")


def _sc_workers():
    info = plsc.get_sparse_core_info()
    return info.num_cores, info.num_cores * info.num_subcores


def _dispatch_rows(hp, dest3, n_rows):
    t, w = hp.shape
    n_chunks, top_k, chunk = dest3.shape
    n_cores, n_workers = _sc_workers()
    per_worker = n_chunks // n_workers

    @functools.partial(
        pl.kernel, mesh=_sc_mesh(),
        out_type=jax.ShapeDtypeStruct((n_rows, w), hp.dtype),
        scratch_types=[pltpu.VMEM((top_k, chunk), I32), pltpu.VMEM((chunk, w), hp.dtype),
                       pltpu.SemaphoreType.DMA],
        name="dispatch_rows")
    def k(hp_hbm, dest_hbm, xs_hbm, idx_v, rows_v, sem):
        wid = lax.axis_index("s") * n_cores + lax.axis_index("c")

        @pl.loop(0, per_worker)
        def _(j):
            c = wid * per_worker + j
            pltpu.sync_copy(dest_hbm.at[c], idx_v)
            pltpu.sync_copy(hp_hbm.at[pl.ds(c * chunk, chunk)], rows_v)
            copies = [pltpu.async_copy(rows_v, xs_hbm.at[idx_v.at[q]], sem) for q in range(top_k)]
            for cp in copies:
                cp.wait()

    return k(hp, dest3)


def _gather_rows(ys, flat_idx):
    n = flat_idx.shape[0]
    w = ys.shape[1]
    n_cores, n_workers = _sc_workers()
    per_worker = n // n_workers
    steps = per_worker // SC_CHUNK

    @functools.partial(
        pl.kernel, mesh=_sc_mesh(),
        out_type=jax.ShapeDtypeStruct((n, w), ys.dtype),
        scratch_types=[pltpu.VMEM((SC_CHUNK,), I32), pltpu.VMEM((SC_CHUNK, w), ys.dtype),
                       pltpu.SemaphoreType.DMA],
        name="gather_rows")
    def k(ys_hbm, idx_hbm, out_hbm, idx_v, rows_v, sem):
        wid = lax.axis_index("s") * n_cores + lax.axis_index("c")

        @pl.loop(0, steps)
        def _(j):
            base = wid * per_worker + j * SC_CHUNK
            pltpu.sync_copy(idx_hbm.at[pl.ds(base, SC_CHUNK)], idx_v)
            pltpu.async_copy(ys_hbm.at[idx_v], rows_v, sem).wait()
            pltpu.sync_copy(rows_v, out_hbm.at[pl.ds(base, SC_CHUNK)])

    return k(ys, flat_idx)


def _experts_kernel(be_ref, nxt_ref, nu_ref, x_ref, wg_hbm, wu_hbm, wd_hbm, o_ref,
                    wg_stage, wu_stage, wd_stage, wg_scr, wu_scr, wd_scr, sem):
    half = wg_scr.shape[0] // 2

    def weight_copies(expert):
        return (pltpu.make_async_copy(wg_hbm.at[expert], wg_stage, sem.at[0]),
                pltpu.make_async_copy(wu_hbm.at[expert], wu_stage, sem.at[1]),
                pltpu.make_async_copy(wd_hbm.at[expert], wd_stage, sem.at[2]))

    def one_block(blk, rows):
        e = be_ref[blk]
        fresh = jnp.logical_or(blk == 0, e != be_ref[jnp.maximum(blk - 1, 0)])

        @pl.when(blk == 0)
        def _():
            for cp in weight_copies(e):
                cp.start()

        @pl.when(fresh)
        def _():
            for cp in weight_copies(e):
                cp.wait()
            wg_scr[...] = wg_stage[...].astype(BF16)
            wu_scr[...] = wu_stage[...].astype(BF16)
            wd_scr[...] = wd_stage[...].astype(BF16)

        @pl.when(jnp.logical_and(fresh, nxt_ref[blk] >= 0))
        def _():
            for cp in weight_copies(nxt_ref[blk]):
                cp.start()

        @pl.when(blk < nu_ref[0])
        def _():
            lo, hi = _unpack_halves(x_ref[rows, :])
            lo = lo.astype(BF16)
            hi = hi.astype(BF16)
            g = _dot(lo, wg_scr[0:half, :]) + _dot(hi, wg_scr[half:, :])
            u = _dot(lo, wu_scr[0:half, :]) + _dot(hi, wu_scr[half:, :])
            act = (g * jax.nn.sigmoid(g) * u).astype(BF16)
            y = _dot(act, wd_scr[...])
            o_ref[rows, :] = _pack_halves(y[:, :half], y[:, half:])

        @pl.when(blk >= nu_ref[0])
        def _():
            o_ref[rows, :] = jnp.zeros((ROW_BLOCK, o_ref.shape[1]), o_ref.dtype)

    for sub in range(STEP_BLOCKS):
        one_block(pl.program_id(0) * STEP_BLOCKS + sub, slice(sub * ROW_BLOCK, (sub + 1) * ROW_BLOCK))


def _experts(block_e, next_e, n_used, xs, w_gate, w_up, w_down):
    n_rows, w = xs.shape
    _, d, f = w_gate.shape
    step_rows = STEP_BLOCKS * ROW_BLOCK
    anywhere = pl.BlockSpec(memory_space=pl.ANY)
    return pl.pallas_call(
        _experts_kernel,
        grid_spec=pltpu.PrefetchScalarGridSpec(
            num_scalar_prefetch=3,
            grid=(n_rows // step_rows,),
            in_specs=[pl.BlockSpec((step_rows, w), lambda i, be, nx, nu: (i, 0)), anywhere, anywhere, anywhere],
            out_specs=pl.BlockSpec((step_rows, w), lambda i, be, nx, nu: (i, 0)),
            scratch_shapes=[pltpu.VMEM((d, f), F32), pltpu.VMEM((d, f), F32), pltpu.VMEM((f, d), F32),
                            pltpu.VMEM((d, f), BF16), pltpu.VMEM((d, f), BF16), pltpu.VMEM((f, d), BF16),
                            pltpu.SemaphoreType.DMA((3,))]),
        out_shape=jax.ShapeDtypeStruct((n_rows, w), I32),
        compiler_params=_cparams("arbitrary"),
        name="experts",
    )(block_e, next_e, n_used, xs, w_gate, w_up, w_down)


def _final_kernel(base_ref, g_ref, wt_ref, gf_ref, ng_ref, *rest):
    o_ref = rest[-1]
    half = base_ref.shape[1] // 2
    wt = wt_ref[...]
    acc_lo = jnp.zeros((base_ref.shape[0], half), F32)
    acc_hi = jnp.zeros((base_ref.shape[0], half), F32)
    for k in range(TOP_K):
        lo, hi = _unpack_halves(g_ref[k])
        wk = wt[:, k:k + 1]
        acc_lo = acc_lo + wk * lo
        acc_hi = acc_hi + wk * hi
    gf = gf_ref[0]
    x_lo = base_ref[:, :half] + gf[:, :half] * acc_lo
    x_hi = base_ref[:, half:] + gf[:, half:] * acc_hi
    ms = (jnp.sum(x_lo * x_lo, axis=-1, keepdims=True)
          + jnp.sum(x_hi * x_hi, axis=-1, keepdims=True)) * (1.0 / (2 * half))
    inv = lax.rsqrt(ms + EPS)
    o_ref[:, :half] = x_lo * inv * ng_ref[:, :half]
    o_ref[:, half:] = x_hi * inv * ng_ref[:, half:]


def _final(base, gathered, wt, gf, norm_g, seq, b0, t_total, prev):
    t, d = base.shape
    tm = FINAL_TILE
    per_seq = seq // tm
    in_specs = [pl.BlockSpec((tm, d), lambda i: (i, 0)),
                pl.BlockSpec((TOP_K, tm, d // 2), lambda i: (0, i, 0)),
                pl.BlockSpec((tm, LANES), lambda i: (i, 0)),
                pl.BlockSpec((1, 1, d), lambda i: (b0 + i // per_seq, 0, 0)),
                pl.BlockSpec((1, d), lambda i: (0, 0))]
    args = [base, gathered, wt, gf, norm_g.reshape(1, d)]
    aliases = {}
    if prev is not None:
        in_specs.append(pl.BlockSpec(memory_space=pl.ANY))
        args.append(prev)
        aliases = {len(args) - 1: 0}
    return pl.pallas_call(
        _final_kernel,
        grid=(t // tm,),
        in_specs=in_specs,
        out_specs=pl.BlockSpec((tm, d), lambda i: (b0 * per_seq + i, 0)),
        out_shape=jax.ShapeDtypeStruct((t_total, d), F32),
        input_output_aliases=aliases,
        compiler_params=_cparams("arbitrary"),
        name="final",
    )(*args)


def _block_table(counts, n_blocks):
    padded = (counts + ROW_BLOCK - 1) // ROW_BLOCK * ROW_BLOCK
    pend = jnp.cumsum(padded)
    pstart = (pend - padded).astype(I32)
    n_used = (pend[-1] // ROW_BLOCK).astype(I32)
    blk = jnp.arange(n_blocks, dtype=I32)
    ended = (pend[None, :] <= (blk * ROW_BLOCK)[:, None]).astype(I32)
    block_e = jnp.minimum(jnp.sum(ended, axis=1), N_EXPERTS - 1).astype(I32)
    last_e = block_e[jnp.maximum(n_used - 1, 0)]
    block_e = jnp.where(blk < n_used, block_e, last_e)
    ex = jnp.arange(N_EXPERTS, dtype=I32)
    later = jnp.logical_and(ex[None, :] > ex[:, None], padded[None, :] > 0)
    nxt = jnp.min(jnp.where(later, ex[None, :], N_EXPERTS), axis=1)
    nxt = jnp.where(nxt == N_EXPERTS, -1, nxt).astype(I32)
    next_e = jnp.sum(jnp.where(block_e[:, None] == ex[None, :], nxt[None, :], 0), axis=1).astype(I32)
    return pstart, block_e, next_e, n_used.reshape(1)


def _group_sizes(bsz):
    if bsz % 8 == 0:
        return (5 * bsz // 8, 3 * bsz // 8)
    if bsz % 2 == 0:
        return (bsz // 2, bsz // 2)
    return (bsz,)


def _moe_group(out_prev, b0, nb, xf, mods, seq, p):
    sh_m, sc_m, g_m, sh_f, sc_f, g_f = mods
    d = xf.shape[1]
    t = nb * seq
    x1 = _mixer(xf, sh_m, sc_m, g_m, p["norm_mix_g"].reshape(1, d), p["w_in"], p["conv_a_w"], p["conv_b_w"],
                p["conv_b_b"], p["ln_b_g"], p["ln_b_b"], p["head_g"], p["w_out"], seq, b0, nb)
    base, hp, idx, rank, wt, counts = _ffn_pre(x1, sh_f, sc_f, g_f, p["norm_ffn_g"].reshape(1, d), p["w_router"],
                                               p["router_bias"], p["w_shared_gate"], p["w_shared_up"],
                                               p["w_shared_down"], seq, b0)
    n_assign = t * TOP_K
    n_blocks = (n_assign + N_EXPERTS * (ROW_BLOCK - 1) + ROW_BLOCK - 1) // ROW_BLOCK
    n_blocks = (n_blocks + STEP_BLOCKS - 1) // STEP_BLOCKS * STEP_BLOCKS
    pstart, block_e, next_e, n_used = _block_table(counts[:, 0], n_blocks)
    dest = _dest(pstart, idx, rank)
    dest3 = dest.reshape(TOP_K, t // SC_CHUNK, SC_CHUNK).transpose(1, 0, 2)
    xs = _dispatch_rows(hp, dest3, n_blocks * ROW_BLOCK)
    ys = _experts(block_e, next_e, n_used, xs, p["w_gate"], p["w_up"], p["w_down"])
    gathered = _gather_rows(ys, dest.reshape(-1)).reshape(TOP_K, t, d // 2)
    return _final(base, gathered, wt, g_f, p["norm_final_g"], seq, b0, xf.shape[0], out_prev)


def _layer(x, c, p):
    bsz, seq, d = x.shape
    xf = x.reshape(bsz * seq, d)
    mod = _adaln(c, p["w_ada"], p["b_ada"])
    mods = [m.reshape(bsz, 1, d) for m in jnp.split(mod, 6, axis=-1)]
    out, b0 = None, 0
    for nb in _group_sizes(bsz):
        out = _moe_group(out, b0, nb, xf, mods, seq, p)
        b0 += nb
    return out.reshape(bsz, seq, d)


def kernel(x, c, w_ada, b_ada, norm_mix_g, w_in, conv_a_w, conv_b_w, conv_b_b, ln_b_g, ln_b_b, head_norm_a_g,
           head_norm_b_g, w_out, norm_ffn_g, w_router, router_bias, w_gate, w_up, w_down, w_shared_gate,
           w_shared_up, w_shared_down, norm_final_g):
    assert w_ada.shape[0] == 1, "the fused final norm assumes a single layer"
    p = dict(w_ada=w_ada[0], b_ada=b_ada[0], norm_mix_g=norm_mix_g[0], w_in=w_in[0].astype(BF16),
             conv_a_w=conv_a_w[0], conv_b_w=conv_b_w[0], conv_b_b=conv_b_b[0], ln_b_g=ln_b_g[0], ln_b_b=ln_b_b[0],
             head_g=jnp.concatenate([head_norm_a_g[0], head_norm_b_g[0]]), w_out=w_out[0].astype(BF16),
             norm_ffn_g=norm_ffn_g[0], w_router=w_router[0], router_bias=router_bias[0], w_gate=w_gate[0],
             w_up=w_up[0], w_down=w_down[0], w_shared_gate=w_shared_gate[0], w_shared_up=w_shared_up[0],
             w_shared_down=w_shared_down[0], norm_final_g=norm_final_g)
    return _layer(x, c, p)
```
